```python
import math
import jax, jax.numpy as jnp
from jax import lax
import numpy as np

D_MODEL = 1024
BATCH = 4
SEQ = 4096
DEPTH = 4

N_HEADS_A = 4
DIFF_QK_DIM = 64
DIFF_V_DIM = 128
N_HEADS_B = 4
SB_DIM = 64
N_GROUPS_C = 4
SGU_DIM = 64
CHUNK = 128
QBLOCK = 128
WIDTH_A = N_HEADS_A * DIFF_V_DIM
WIDTH_B = N_HEADS_B * SB_DIM
WIDTH_C = N_GROUPS_C * SGU_DIM
D_MIX = WIDTH_A + WIDTH_B + WIDTH_C
IN_WIDTHS = (N_HEADS_A * 2 * DIFF_QK_DIM, N_HEADS_A * 2 * DIFF_QK_DIM, WIDTH_A,
             WIDTH_B, WIDTH_B, WIDTH_B, WIDTH_C, WIDTH_C)
D_IN = 2816
ROPE_THETA = 500000.0
ROPE_DIM = DIFF_QK_DIM // 4
MAX_POS_OFFSET = 1024
MEM_LEN = 256
CROSS_HEADS = 4
CROSS_DIM = 64
CROSS_WIDTH = CROSS_HEADS * CROSS_DIM
N_EXPERT_GROUPS = 4
EXPERTS_PER_GROUP = 4
N_EXPERTS = N_EXPERT_GROUPS * EXPERTS_PER_GROUP
TOP_K_IN_GROUP = 2
D_EXPERT = 512
EPS = 1e-6

kernel_name = "hybrid_diff_stickbreak_sgu_hmoe"


def rmsnorm(x, g):
    xf = x.astype(jnp.float32)
    y = xf * lax.rsqrt(jnp.mean(xf * xf, axis=-1, keepdims=True) + EPS)
    return (y * g.astype(jnp.float32)).astype(x.dtype)


def rotary_tables(positions):
    inv_freq = ROPE_THETA ** (-jnp.arange(0, ROPE_DIM, 2, dtype=jnp.float32) / ROPE_DIM)
    ang = positions.astype(jnp.float32)[..., None] * inv_freq
    return jnp.cos(ang)[:, :, None, None, :], jnp.sin(ang)[:, :, None, None, :]


def partial_rotary(t, cos, sin):
    half = ROPE_DIM // 2
    t1 = t[..., :half].astype(jnp.float32)
    t2 = t[..., half:ROPE_DIM].astype(jnp.float32)
    rot = jnp.concatenate([t1 * cos - t2 * sin, t2 * cos + t1 * sin], axis=-1).astype(t.dtype)
    return jnp.concatenate([rot, t[..., ROPE_DIM:]], axis=-1)


def diff_attention(q, k, v, lam, subln_g, lam_init):
    B, S, H, _, Dm = q.shape
    nb = S // QBLOCK
    qb = q.reshape(B, nb, QBLOCK, H, 2, Dm).transpose(1, 0, 3, 4, 2, 5)
    kt = k.transpose(0, 2, 3, 1, 4)
    vt = v.transpose(0, 2, 1, 3)
    key_pos = jnp.arange(S)
    scale = Dm ** -0.5

    def block(args):
        qi, start = args
        s = jnp.einsum('bhmqd,bhmkd->bhmqk', qi, kt).astype(jnp.float32) * scale
        qpos = start + jnp.arange(QBLOCK)
        causal = key_pos[None, :] <= qpos[:, None]
        p = jax.nn.softmax(jnp.where(causal, s, -jnp.inf), axis=-1)
        a = p[:, :, 0] - lam * p[:, :, 1]
        return jnp.einsum('bhqk,bhkd->bhqd', a.astype(vt.dtype), vt)

    o = lax.map(block, (qb, jnp.arange(nb) * QBLOCK))
    o = o.transpose(1, 0, 3, 2, 4).reshape(B, S, H, v.shape[-1])
    o = rmsnorm(o, subln_g) * (1.0 - lam_init)
    return o.reshape(B, S, H * v.shape[-1])


def stick_breaking(q, k, v):
    B, S, H, D = q.shape
    nb = S // QBLOCK
    qb = q.reshape(B, nb, QBLOCK, H, D).transpose(1, 0, 3, 2, 4)
    kt = k.transpose(0, 2, 1, 3)
    vt = v.transpose(0, 2, 1, 3)
    key_pos = jnp.arange(S)
    scale = D ** -0.5

    def block(args):
        qi, start = args
        z = jnp.einsum('bhqd,bhkd->bhqk', qi, kt).astype(jnp.float32) * scale
        qpos = start + jnp.arange(QBLOCK)
        strict = key_pos[None, :] < qpos[:, None]
        log_keep = jnp.where(strict, jax.nn.log_sigmoid(-z), 0.0)
        after = lax.cumsum(log_keep, axis=3, reverse=True) - log_keep
        w = jnp.where(strict, jnp.exp(jax.nn.log_sigmoid(z) + after), 0.0)
        return jnp.einsum('bhqk,bhkd->bhqd', w.astype(vt.dtype), vt)

    o = lax.map(block, (qb, jnp.arange(nb) * QBLOCK))
    return o.transpose(1, 0, 3, 2, 4).reshape(B, S, H, D)


def chunk_spatial_gating(u, v, sgu_g, w_s, b_s):
    B, S, _ = v.shape
    nc = S // CHUNK
    vn = rmsnorm(v.reshape(B, S, N_GROUPS_C, SGU_DIM), sgu_g.reshape(N_GROUPS_C, SGU_DIM))
    vc = vn.reshape(B, nc, CHUNK, N_GROUPS_C, SGU_DIM)
    causal = jnp.tril(jnp.ones((CHUNK, CHUNK), dtype=bool))
    ws = jnp.where(causal[None], w_s, 0.0)
    s = jnp.einsum('gts,bnsgc->bntgc', ws, vc) + b_s.T[None, None, :, :, None]
    out = u.reshape(B, nc, CHUNK, N_GROUPS_C, SGU_DIM) * s
    return out.reshape(B, S, N_GROUPS_C, SGU_DIM)


def memory_cross_attention(h, m, w_cq, w_ck, w_cv, w_co):
    B, S, _ = h.shape
    M = m.shape[1]
    q = (h @ w_cq).reshape(B, S, CROSS_HEADS, CROSS_DIM)
    k = (m @ w_ck).reshape(B, M, CROSS_HEADS, CROSS_DIM)
    v = (m @ w_cv).reshape(B, M, CROSS_HEADS, CROSS_DIM)
    s = jnp.einsum('bqhd,bkhd->bhqk', q, k).astype(jnp.float32) * (CROSS_DIM ** -0.5)
    p = jax.nn.softmax(s, axis=-1).astype(v.dtype)
    o = jnp.einsum('bhqk,bkhd->bqhd', p, v).reshape(B, S, CROSS_WIDTH)
    return o @ w_co


def hierarchical_moe(h, w_group, b_group, w_router, b_router, w_gate, w_up, w_down):
    B, S, _ = h.shape
    gl = (h @ w_group).astype(jnp.float32) + b_group.astype(jnp.float32)
    gp = jax.nn.softmax(gl, axis=-1)
    _, gidx = lax.top_k(gl, 1)
    p_group = jnp.take_along_axis(gp, gidx, axis=-1)
    el = ((h @ w_router).astype(jnp.float32) + b_router.astype(jnp.float32)).reshape(
        B, S, N_EXPERT_GROUPS, EXPERTS_PER_GROUP)
    sel = jnp.broadcast_to(gidx[..., None], (B, S, 1, EXPERTS_PER_GROUP))
    el_sel = jnp.take_along_axis(el, sel, axis=2)[:, :, 0, :]
    ev, eidx = lax.top_k(el_sel, TOP_K_IN_GROUP)
    w_sel = jax.nn.softmax(ev, axis=-1) * p_group
    expert_ids = gidx * EXPERTS_PER_GROUP + eidx
    gates = jnp.sum(jax.nn.one_hot(expert_ids, N_EXPERTS, dtype=jnp.float32) * w_sel[..., None], axis=2)
    gates = gates.astype(h.dtype)
    y = jnp.zeros_like(h)
    for e in range(N_EXPERTS):
        act = jax.nn.silu(h @ w_gate[e]) * (h @ w_up[e])
        y = y + gates[..., e:e + 1] * (act @ w_down[e])
    return y


def setup_inputs(seed: int = 0) -> dict:
    key = jax.random.key(seed)
    ks = jax.random.split(key, 40)
    cnt = [0]

    def nk():
        cnt[0] += 1
        return ks[cnt[0] - 1]

    def nrm(shape, scale):
        return jax.random.normal(nk(), shape, jnp.float32) * scale

    def gain(shape):
        return 1.0 + 0.02 * jax.random.normal(nk(), shape, jnp.float32)

    L, D = DEPTH, D_MODEL
    x = nrm((BATCH, SEQ, D), 1.0)
    mem = nrm((BATCH, MEM_LEN, D), 1.0)
    positions = (jax.random.randint(nk(), (BATCH, 1), 0, MAX_POS_OFFSET, dtype=jnp.int32)
                 + jnp.arange(SEQ, dtype=jnp.int32)[None, :])
    return {
        "x": x,
        "mem": mem,
        "positions": positions,
        "norm_mix": gain((L, D)),
        "w_in": nrm((L, D, D_IN), D ** -0.5),
        "lam_q1": nrm((L, DIFF_QK_DIM), 0.1),
        "lam_k1": nrm((L, DIFF_QK_DIM), 0.1),
        "lam_q2": nrm((L, DIFF_QK_DIM), 0.1),
        "lam_k2": nrm((L, DIFF_QK_DIM), 0.1),
        "subln_a": gain((L, DIFF_V_DIM)),
        "norm_b_out": gain((L, WIDTH_B)),
        "sgu_norm": gain((L, WIDTH_C)),
        "w_spatial": nrm((L, N_GROUPS_C, CHUNK, CHUNK), CHUNK ** -0.5),
        "b_spatial": gain((L, N_GROUPS_C, CHUNK)),
        "norm_c_out": gain((L, WIDTH_C)),
        "w_out": nrm((L, D_MIX, D), D_MIX ** -0.5),
        "norm_cross": gain((L, D)),
        "norm_mem": gain((L, D)),
        "w_cq": nrm((L, D, CROSS_WIDTH), D ** -0.5),
        "w_ck": nrm((L, D, CROSS_WIDTH), D ** -0.5),
        "w_cv": nrm((L, D, CROSS_WIDTH), D ** -0.5),
        "w_co": nrm((L, CROSS_WIDTH, D), CROSS_WIDTH ** -0.5),
        "norm_ffn": gain((L, D)),
        "w_group": nrm((L, D, N_EXPERT_GROUPS), D ** -0.5),
        "b_group": nrm((L, N_EXPERT_GROUPS), 0.01),
        "w_router": nrm((L, D, N_EXPERTS), D ** -0.5),
        "b_router": nrm((L, N_EXPERTS), 0.01),
        "w_gate": nrm((L, N_EXPERTS, D, D_EXPERT), D ** -0.5),
        "w_up": nrm((L, N_EXPERTS, D, D_EXPERT), D ** -0.5),
        "w_down": nrm((L, N_EXPERTS, D_EXPERT, D), D_EXPERT ** -0.5),
        "norm_final": gain((D,)),
    }


def reference(x, mem, positions, norm_mix, w_in, lam_q1, lam_k1, lam_q2, lam_k2, subln_a,
              norm_b_out, sgu_norm, w_spatial, b_spatial, norm_c_out, w_out, norm_cross, norm_mem,
              w_cq, w_ck, w_cv, w_co, norm_ffn, w_group, b_group, w_router, b_router,
              w_gate, w_up, w_down, norm_final):
    B, S, _ = x.shape
    cos, sin = rotary_tables(positions)
    split_at = np.cumsum(IN_WIDTHS)[:-1].tolist()
    for l in range(DEPTH):
        lam_init = 0.8 - 0.6 * math.exp(-0.3 * l)
        h = rmsnorm(x, norm_mix[l])
        proj = h @ w_in[l]
        aq, ak, av, bq, bk, bv, cu, cv = jnp.split(proj, split_at, axis=-1)
        aq = partial_rotary(aq.reshape(B, S, N_HEADS_A, 2, DIFF_QK_DIM), cos, sin)
        ak = partial_rotary(ak.reshape(B, S, N_HEADS_A, 2, DIFF_QK_DIM), cos, sin)
        av = av.reshape(B, S, N_HEADS_A, DIFF_V_DIM)
        lam = (jnp.exp(jnp.sum(lam_q1[l].astype(jnp.float32) * lam_k1[l].astype(jnp.float32)))
               - jnp.exp(jnp.sum(lam_q2[l].astype(jnp.float32) * lam_k2[l].astype(jnp.float32)))
               + lam_init)
        ya = diff_attention(aq, ak, av, lam, subln_a[l], lam_init)
        yb = stick_breaking(bq.reshape(B, S, N_HEADS_B, SB_DIM), bk.reshape(B, S, N_HEADS_B, SB_DIM),
                            bv.reshape(B, S, N_HEADS_B, SB_DIM))
        yb = rmsnorm(yb, norm_b_out[l].reshape(N_HEADS_B, SB_DIM)).reshape(B, S, WIDTH_B)
        yc = chunk_spatial_gating(jax.nn.gelu(cu), jax.nn.gelu(cv), sgu_norm[l], w_spatial[l], b_spatial[l])
        yc = rmsnorm(yc, norm_c_out[l].reshape(N_GROUPS_C, SGU_DIM)).reshape(B, S, WIDTH_C)
        mix = jnp.concatenate([ya, yb, yc], axis=-1)
        x = x + mix @ w_out[l]
        x = x + memory_cross_attention(rmsnorm(x, norm_cross[l]), rmsnorm(mem, norm_mem[l]),
                                       w_cq[l], w_ck[l], w_cv[l], w_co[l])
        x = x + hierarchical_moe(rmsnorm(x, norm_ffn[l]), w_group[l], b_group[l], w_router[l],
                                 b_router[l], w_gate[l], w_up[l], w_down[l])
    return rmsnorm(x, norm_final)
```

```python
import functools
import math

import jax
import jax.numpy as jnp
from jax import lax
from jax.experimental import pallas as pl
from jax.experimental.pallas import tpu as pltpu

F32 = jnp.float32
BF16 = jnp.bfloat16
EPS = 1e-6

D_MODEL = 1024
N_HEADS_A = 4
DIFF_QK_DIM = 64
N_HEADS_B = 4
SB_DIM = 64
N_GROUPS_C = 4
CHUNK = 128
WIDTH_A = 512
WIDTH_B = 256
WIDTH_C = 256
D_IN = 2816
ROPE_THETA = 500000.0
ROPE_DIM = 16
MEM_LEN = 256
CROSS_WIDTH = 256
N_EXPERT_GROUPS = 4
EXPERTS_PER_GROUP = 4
N_EXPERTS = 16
D_EXPERT = 512
N_CLASSES = 24
CLASS_ROWS = 32
ROUTER_LANES = 128
PAIR_LO = (0, 0, 0, 1, 1, 2)
PAIR_HI = (1, 2, 3, 2, 3, 3)

ROW_TILE = 512
ATT_TILE = 256
MOE_TILE = 256
VMEM_LIMIT = 56 * 1024 * 1024


def _rms(x, g):
    return x * lax.rsqrt(jnp.mean(x * x, axis=-1, keepdims=True) + EPS) * g


def _group_rms64(x, g):
    m, w = x.shape
    grp = lax.broadcasted_iota(jnp.int32, (m, w), 1) // 64
    sq = x * x
    scale = jnp.zeros_like(x)
    for gi in range(w // 64):
        sel = grp == gi
        ss = jnp.sum(jnp.where(sel, sq, 0.0), axis=-1, keepdims=True)
        scale = jnp.where(sel, lax.rsqrt(ss * (1.0 / 64.0) + EPS), scale)
    return x * scale * g


def _nt_dot(a, b, **kw):
    return lax.dot_general(a, b, (((1,), (1,)), ((), ())), preferred_element_type=F32, **kw)


def _inproj_kernel(x_ref, g_ref, w_ref, c_ref, s1_ref, s2_ref,
                   aq_ref, ak_ref, av_ref, bq_ref, bk_ref, bv_ref, cu_ref, cv_ref):
    h = _rms(x_ref[...], g_ref[...]).astype(BF16)

    def proj(lo, hi):
        return jnp.dot(h, w_ref[:, lo:hi], preferred_element_type=F32)

    c, s1, s2 = c_ref[...], s1_ref[...], s2_ref[...]

    def rot(t):
        outs = []
        for hh in range(4):
            th = t[:, hh * 128:(hh + 1) * 128]
            outs.append(th * c + pltpu.roll(th, 120, 1) * s1 + pltpu.roll(th, 8, 1) * s2)
        return jnp.concatenate(outs, axis=1)

    aq_ref[...] = (rot(proj(0, 512)) * 0.125).astype(BF16)
    ak_ref[...] = rot(proj(512, 1024)).astype(BF16)
    av_ref[...] = proj(1024, 1536).astype(BF16)
    bq_ref[...] = (proj(1536, 1792) * 0.125).astype(BF16)
    bk_ref[...] = proj(1792, 2048).astype(BF16)
    bv_ref[...] = proj(2048, 2304).astype(BF16)
    cu_ref[...] = jax.nn.gelu(proj(2304, 2560))
    cv_ref[...] = jax.nn.gelu(proj(2560, 2816))


def _inproj(x2, g, w_bf, rc, rs1, rs2):
    t = x2.shape[0]
    tm = ROW_TILE
    row = lambda w: pl.BlockSpec((tm, w), lambda i: (i, 0))
    full = lambda a: pl.BlockSpec(a.shape, lambda i: (0,) * a.ndim)
    outs = [jax.ShapeDtypeStruct((t, 512), BF16)] * 3 + [jax.ShapeDtypeStruct((t, 256), BF16)] * 3 \
        + [jax.ShapeDtypeStruct((t, 256), F32)] * 2
    return pl.pallas_call(
        _inproj_kernel,
        grid=(t // tm,),
        in_specs=[row(D_MODEL), full(g), full(w_bf), row(128), row(128), row(128)],
        out_specs=[row(512)] * 3 + [row(256)] * 5,
        out_shape=outs,
        compiler_params=pltpu.CompilerParams(dimension_semantics=("parallel",), vmem_limit_bytes=VMEM_LIMIT),
        name="inproj",
    )(x2, g, w_bf, rc, rs1, rs2)


def _diff_kernel(lam_ref, g_ref, q_ref, k_ref, v_ref, o_ref, qs_ref, m_ref, l_ref, acc_ref, *, tq, lam_init):
    i = pl.program_id(2)
    q = q_ref[...]
    lane = lax.broadcasted_iota(jnp.int32, (tq, 128), 1)
    zero = jnp.zeros_like(q)
    qs_ref[0:tq, :] = jnp.where(lane < 64, q, zero)
    qs_ref[tq:, :] = jnp.where(lane >= 64, q, zero)
    m_ref[...] = jnp.full_like(m_ref, -jnp.inf)
    l_ref[...] = jnp.zeros_like(l_ref)
    acc_ref[...] = jnp.zeros_like(acc_ref)

    def step(j, masked):
        off = pl.multiple_of(j * tq, tq)
        k = k_ref[pl.ds(off, tq), :]
        v = v_ref[pl.ds(off, tq), :]
        s = _nt_dot(qs_ref[...], k)
        if masked:
            r = lax.broadcasted_iota(jnp.int32, (2 * tq, tq), 0)
            cidx = lax.broadcasted_iota(jnp.int32, (2 * tq, tq), 1)
            r = jnp.where(r >= tq, r - tq, r)
            s = jnp.where(cidx <= r, s, -jnp.inf)
        m_old = m_ref[...]
        m_new = jnp.maximum(m_old, jnp.max(s, axis=-1, keepdims=True))
        alpha = jnp.exp(m_old - m_new)
        p = jnp.exp(s - m_new)
        l_ref[...] = alpha * l_ref[...] + jnp.sum(p, axis=-1, keepdims=True)
        acc_ref[...] = alpha * acc_ref[...] + jnp.dot(p.astype(BF16), v, preferred_element_type=F32)
        m_ref[...] = m_new

    def body(j, c):
        step(j, False)
        return c

    lax.fori_loop(0, i, body, 0)
    step(i, True)

    lp = lam_ref[...]
    lam = (jnp.exp(jnp.sum(lp[0:1] * lp[1:2], axis=-1, keepdims=True))
           - jnp.exp(jnp.sum(lp[2:3] * lp[3:4], axis=-1, keepdims=True)) + lam_init)
    o = acc_ref[...] / l_ref[...]
    o = o[0:tq] - lam * o[tq:]
    o_ref[...] = (_rms(o, g_ref[...]) * (1.0 - lam_init)).astype(o_ref.dtype)


def _diff_attention(aq, ak, av, lam_p, subln_g, lam_init, batch, seq):
    tq = ATT_TILE
    nq = seq // tq
    kern = functools.partial(_diff_kernel, tq=tq, lam_init=lam_init)
    qspec = pl.BlockSpec((tq, 128), lambda b, h, i: (b * nq + i, h))
    kvspec = pl.BlockSpec((seq, 128), lambda b, h, i: (b, h))
    return pl.pallas_call(
        kern,
        grid=(batch, N_HEADS_A, nq),
        in_specs=[pl.BlockSpec(lam_p.shape, lambda b, h, i: (0, 0)),
                  pl.BlockSpec(subln_g.shape, lambda b, h, i: (0, 0)),
                  qspec, kvspec, kvspec],
        out_specs=qspec,
        out_shape=jax.ShapeDtypeStruct(aq.shape, BF16),
        scratch_shapes=[pltpu.VMEM((2 * tq, 128), BF16), pltpu.VMEM((2 * tq, 1), F32),
                        pltpu.VMEM((2 * tq, 1), F32), pltpu.VMEM((2 * tq, 128), F32)],
        compiler_params=pltpu.CompilerParams(dimension_semantics=("parallel", "parallel", "arbitrary"),
                                             vmem_limit_bytes=VMEM_LIMIT),
        name="diff_attn",
    )(lam_p, subln_g, aq, ak, av)


def _sb_kernel(g_ref, u_ref, q_ref, k_ref, v_ref, o_ref, qs_ref, c_ref, acc_ref, *, tq):
    i = pl.program_id(2)
    q = q_ref[...]
    lane = lax.broadcasted_iota(jnp.int32, (tq, 128), 1)
    zero = jnp.zeros_like(q)
    qs_ref[0:tq, :] = jnp.where(lane < 64, q, zero)
    qs_ref[tq:, :] = jnp.where(lane >= 64, q, zero)
    c_ref[...] = jnp.zeros_like(c_ref)
    acc_ref[...] = jnp.zeros_like(acc_ref)

    def step(j, masked):
        off = pl.multiple_of(j * tq, tq)
        k = k_ref[pl.ds(off, tq), :]
        v = v_ref[pl.ds(off, tq), :]
        z = _nt_dot(qs_ref[...], k)
        sp = jnp.maximum(z, 0.0) + jnp.log(1.0 + jnp.exp(-jnp.abs(z)))
        if masked:
            r = lax.broadcasted_iota(jnp.int32, (2 * tq, tq), 0)
            cidx = lax.broadcasted_iota(jnp.int32, (2 * tq, tq), 1)
            r = jnp.where(r >= tq, r - tq, r)
            strict = cidx < r
            sp = jnp.where(strict, sp, 0.0)
        hi = sp.astype(BF16)
        lo = (sp - hi.astype(F32)).astype(BF16)
        u = u_ref[...]
        cum = jnp.dot(hi, u, preferred_element_type=F32) + jnp.dot(lo, u, preferred_element_type=F32)
        w = jnp.exp(z - sp - cum - c_ref[...])
        if masked:
            w = jnp.where(strict, w, 0.0)
        acc_ref[...] += jnp.dot(w.astype(BF16), v, preferred_element_type=F32)
        c_ref[...] += jnp.sum(sp, axis=-1, keepdims=True)

    step(i, True)

    def body(jj, c):
        step(i - 1 - jj, False)
        return c

    lax.fori_loop(0, i, body, 0)

    acc = acc_ref[...]
    o = jnp.where(lane < 64, acc[0:tq], acc[tq:])
    o_ref[...] = _group_rms64(o, g_ref[...]).astype(o_ref.dtype)


def _stick_breaking(bq, bk, bv, norm_g, batch, seq):
    tq = ATT_TILE
    nq = seq // tq
    u = (lax.broadcasted_iota(jnp.int32, (tq, tq), 0) > lax.broadcasted_iota(jnp.int32, (tq, tq), 1)).astype(BF16)
    kern = functools.partial(_sb_kernel, tq=tq)
    qspec = pl.BlockSpec((tq, 128), lambda b, h, i: (b * nq + i, h))
    kvspec = pl.BlockSpec((seq, 128), lambda b, h, i: (b, h))
    return pl.pallas_call(
        kern,
        grid=(batch, N_HEADS_B // 2, nq),
        in_specs=[pl.BlockSpec((1, 128), lambda b, h, i: (0, h)),
                  pl.BlockSpec((tq, tq), lambda b, h, i: (0, 0)),
                  qspec, kvspec, kvspec],
        out_specs=qspec,
        out_shape=jax.ShapeDtypeStruct(bq.shape, BF16),
        scratch_shapes=[pltpu.VMEM((2 * tq, 128), BF16), pltpu.VMEM((2 * tq, 1), F32),
                        pltpu.VMEM((2 * tq, 128), F32)],
        compiler_params=pltpu.CompilerParams(dimension_semantics=("parallel", "parallel", "arbitrary"),
                                             vmem_limit_bytes=VMEM_LIMIT),
        name="stick_breaking",
    )(norm_g, u, bq, bk, bv)


def _mixout_kernel(x_ref, ya_ref, yb_ref, cu_ref, cv_ref, sg_ref, ws_ref, bs_ref, nc_ref, wo_ref, o_ref, *, tm):
    vn = _group_rms64(cv_ref[...], sg_ref[...]).astype(BF16)
    tril = lax.broadcasted_iota(jnp.int32, (CHUNK, CHUNK), 0) >= lax.broadcasted_iota(jnp.int32, (CHUNK, CHUNK), 1)
    ws = [jnp.where(tril, ws_ref[g], 0.0).astype(BF16) for g in range(N_GROUPS_C)]
    grp = lax.broadcasted_iota(jnp.int32, (CHUNK, WIDTH_C), 1) // 64
    bias = bs_ref[...]
    chunks = []
    for n in range(tm // CHUNK):
        vc = vn[n * CHUNK:(n + 1) * CHUNK, :]
        s = jnp.zeros((CHUNK, WIDTH_C), F32)
        for g in range(N_GROUPS_C):
            s = jnp.where(grp == g, jnp.dot(ws[g], vc, preferred_element_type=F32), s)
        chunks.append(s + bias)
    yc = cu_ref[...] * jnp.concatenate(chunks, axis=0)
    yc = _group_rms64(yc, nc_ref[...]).astype(BF16)
    o_ref[...] = (x_ref[...]
                  + jnp.dot(ya_ref[...], wo_ref[0:512, :], preferred_element_type=F32)
                  + jnp.dot(yb_ref[...], wo_ref[512:768, :], preferred_element_type=F32)
                  + jnp.dot(yc, wo_ref[768:1024, :], preferred_element_type=F32))


def _mixout(x2, ya, yb, cu, cv, sgu_g, w_s, b_full, nc_g, wo_bf):
    t = x2.shape[0]
    tm = ROW_TILE
    row = lambda w: pl.BlockSpec((tm, w), lambda i: (i, 0))
    full = lambda a: pl.BlockSpec(a.shape, lambda i: (0,) * a.ndim)
    return pl.pallas_call(
        functools.partial(_mixout_kernel, tm=tm),
        grid=(t // tm,),
        in_specs=[row(D_MODEL), row(512), row(256), row(256), row(256),
                  full(sgu_g), full(w_s), full(b_full), full(nc_g), full(wo_bf)],
        out_specs=row(D_MODEL),
        out_shape=jax.ShapeDtypeStruct(x2.shape, F32),
        compiler_params=pltpu.CompilerParams(dimension_semantics=("parallel",), vmem_limit_bytes=VMEM_LIMIT),
        name="sgu_outproj",
    )(x2, ya, yb, cu, cv, sgu_g, w_s, b_full, nc_g, wo_bf)


def _memkv_kernel(m_ref, g_ref, wk_ref, wv_ref, k_ref, v_ref):
    h = _rms(m_ref[0], g_ref[0]).astype(BF16)
    k_ref[0, 0] = jnp.dot(h, wk_ref[0].astype(BF16), preferred_element_type=F32).astype(BF16)
    v_ref[0, 0] = jnp.dot(h, wv_ref[0].astype(BF16), preferred_element_type=F32).astype(BF16)


def _memkv(mem, norm_mem, w_ck, w_cv):
    depth = w_ck.shape[0]
    batch = mem.shape[0]
    out = jax.ShapeDtypeStruct((depth, batch, MEM_LEN, CROSS_WIDTH), BF16)
    wspec = pl.BlockSpec((1, D_MODEL, CROSS_WIDTH), lambda l, b: (l, 0, 0))
    ospec = pl.BlockSpec((1, 1, MEM_LEN, CROSS_WIDTH), lambda l, b: (l, b, 0, 0))
    return pl.pallas_call(
        _memkv_kernel,
        grid=(depth, batch),
        in_specs=[pl.BlockSpec((1, MEM_LEN, D_MODEL), lambda l, b: (b, 0, 0)),
                  pl.BlockSpec((1, 1, D_MODEL), lambda l, b: (l, 0, 0)), wspec, wspec],
        out_specs=[ospec, ospec],
        out_shape=[out, out],
        compiler_params=pltpu.CompilerParams(dimension_semantics=("parallel", "parallel"),
                                             vmem_limit_bytes=VMEM_LIMIT),
        name="mem_kv",
    )(mem, norm_mem.reshape(depth, 1, D_MODEL), w_ck, w_cv)


def _cross_kernel(x_ref, nc_ref, wq_ref, km_ref, vm_ref, wo_ref, nf_ref, wrt_ref, brt_ref, u_ref,
                  o_ref, cls_ref, rank_ref, cnt_ref, carry_ref, *, tm):
    @pl.when(pl.program_id(0) == 0)
    def _():
        carry_ref[...] = jnp.zeros_like(carry_ref)

    x = x_ref[...]
    h = _rms(x, nc_ref[...]).astype(BF16)
    q = (jnp.dot(h, wq_ref[...], preferred_element_type=F32) * 0.125).astype(BF16)
    km = km_ref[0]
    vm = vm_ref[0]
    qgrp = lax.broadcasted_iota(jnp.int32, (tm, CROSS_WIDTH), 1) // 64
    vgrp = lax.broadcasted_iota(jnp.int32, (MEM_LEN, CROSS_WIDTH), 1) // 64
    o = jnp.zeros((tm, CROSS_WIDTH), F32)
    for hh in range(4):
        s = _nt_dot(jnp.where(qgrp == hh, q, jnp.zeros_like(q)), km)
        p = jnp.exp(s - jnp.max(s, axis=-1, keepdims=True))
        p = p / jnp.sum(p, axis=-1, keepdims=True)
        o = o + jnp.dot(p.astype(BF16), jnp.where(vgrp == hh, vm, jnp.zeros_like(vm)),
                        preferred_element_type=F32)
    x2 = x + jnp.dot(o.astype(BF16), wo_ref[...], preferred_element_type=F32)
    o_ref[...] = x2

    hf = _rms(x2, nf_ref[...])
    lt = _nt_dot(wrt_ref[...], hf, precision=lax.Precision.HIGHEST) + brt_ref[...]
    gl = [lt[r:r + 1, :] for r in range(4)]
    el = [lt[4 + r:5 + r, :] for r in range(16)]

    def first_argmax(vals):
        mx = functools.reduce(jnp.maximum, vals)
        idx = jnp.full(vals[0].shape, len(vals) - 1, jnp.int32)
        for r in range(len(vals) - 2, -1, -1):
            idx = jnp.where(vals[r] == mx, r, idx)
        return idx

    g = first_argmax(gl)
    es = [jnp.where(g == 0, el[r], jnp.where(g == 1, el[4 + r], jnp.where(g == 2, el[8 + r], el[12 + r])))
          for r in range(4)]
    i1 = first_argmax(es)
    i2 = first_argmax([jnp.where(i1 == r, -jnp.inf, es[r]) for r in range(4)])
    a = jnp.minimum(i1, i2)
    b = jnp.maximum(i1, i2)
    cls = g * 6 + jnp.where(a == 0, b - 1, jnp.where(a == 1, b + 1, 5))

    onehot = lax.broadcasted_iota(jnp.int32, (CLASS_ROWS, tm), 0) == cls
    oh = jnp.where(onehot, 1.0, 0.0)
    before = jnp.dot(oh.astype(BF16), u_ref[...], preferred_element_type=F32) + carry_ref[...]
    rank = jnp.sum(jnp.where(onehot, before, 0.0), axis=0, keepdims=True)
    carry_ref[...] += jnp.sum(oh, axis=1, keepdims=True)
    cls_ref[0] = cls
    rank_ref[0] = rank.astype(jnp.int32)
    cnt_ref[...] = jnp.broadcast_to(carry_ref[...], cnt_ref.shape).astype(jnp.int32)


def _cross_router(x2, nc_g, wq_bf, kmem, vmem, wo_bf, nf_g, wrt_t, brt_col, seq):
    t = x2.shape[0]
    tm = ROW_TILE
    nt = t // tm
    per_b = seq // tm
    u = (lax.broadcasted_iota(jnp.int32, (tm, tm), 0) < lax.broadcasted_iota(jnp.int32, (tm, tm), 1)).astype(BF16)
    row = pl.BlockSpec((tm, D_MODEL), lambda i: (i, 0))
    full = lambda a: pl.BlockSpec(a.shape, lambda i: (0,) * a.ndim)
    mspec = pl.BlockSpec((1, MEM_LEN, CROSS_WIDTH), lambda i: (i // per_b, 0, 0))
    tokspec = pl.BlockSpec((1, 1, tm), lambda i: (i, 0, 0))
    return pl.pallas_call(
        functools.partial(_cross_kernel, tm=tm),
        grid=(nt,),
        in_specs=[row, full(nc_g), full(wq_bf), mspec, mspec, full(wo_bf), full(nf_g), full(wrt_t),
                  full(brt_col), full(u)],
        out_specs=[row, tokspec, tokspec, pl.BlockSpec((CLASS_ROWS, 128), lambda i: (0, 0))],
        out_shape=[jax.ShapeDtypeStruct(x2.shape, F32), jax.ShapeDtypeStruct((nt, 1, tm), jnp.int32),
                   jax.ShapeDtypeStruct((nt, 1, tm), jnp.int32),
                   jax.ShapeDtypeStruct((CLASS_ROWS, 128), jnp.int32)],
        scratch_shapes=[pltpu.VMEM((CLASS_ROWS, 1), F32)],
        compiler_params=pltpu.CompilerParams(dimension_semantics=("arbitrary",), vmem_limit_bytes=VMEM_LIMIT),
        name="cross_router",
    )(x2, nc_g, wq_bf, kmem, vmem, wo_bf, nf_g, wrt_t, brt_col, u)


def _moe_kernel(e1_ref, e2_ref, nv_ref, src_ref,
                x_hbm, nf_ref, wrt_ref, brt_ref, wg1_ref, wu1_ref, wd1_ref, wg2_ref, wu2_ref, wd2_ref,
                nfin_ref, out_hbm, xbuf, obuf, gsem, ssem, *, tile, n_tiles, final):
    k = pl.program_id(0)
    slot = k % 2

    def gather_copy(kk, sl, r):
        tok = src_ref[kk * tile + r]
        return pltpu.make_async_copy(x_hbm.at[pl.ds(tok, 1), :], xbuf.at[sl, pl.ds(r, 1), :], gsem.at[sl])

    def scatter_copy(kk, sl, r):
        tok = src_ref[kk * tile + r]
        return pltpu.make_async_copy(obuf.at[sl, pl.ds(r, 1), :], out_hbm.at[pl.ds(tok, 1), :], ssem.at[sl])

    def for_rows(n, fn):
        def body(r, c):
            fn(r)
            return c
        lax.fori_loop(0, n, body, 0)

    @pl.when(jnp.logical_and(k == 0, nv_ref[0] > 0))
    def _():
        for_rows(tile, lambda r: gather_copy(0, 0, r).start())

    @pl.when(k >= 2)
    def _():
        for_rows(nv_ref[jnp.maximum(k - 2, 0)], lambda r: scatter_copy(k - 2, slot, r).wait())

    nxt = jnp.minimum(k + 1, n_tiles - 1)

    @pl.when(jnp.logical_and(k + 1 < n_tiles, nv_ref[nxt] > 0))
    def _():
        for_rows(tile, lambda r: gather_copy(k + 1, 1 - slot, r).start())

    @pl.when(nv_ref[k] > 0)
    def _():
        for_rows(tile, lambda r: gather_copy(k, slot, r).wait())
        x = xbuf[slot]
        hf = _rms(x, nf_ref[...])
        h = hf.astype(BF16)
        lg = jnp.dot(hf, wrt_ref[...], preferred_element_type=F32, precision=lax.Precision.HIGHEST) + brt_ref[...]
        lane = lax.broadcasted_iota(jnp.int32, lg.shape, 1)
        e1 = e1_ref[k]
        e2 = e2_ref[k]
        glm = jnp.where(lane < N_EXPERT_GROUPS, lg, -jnp.inf)
        eg = jnp.exp(glm - jnp.max(glm, axis=-1, keepdims=True))
        pick = lambda a, idx: jnp.sum(jnp.where(lane == idx, a, 0.0), axis=-1, keepdims=True)
        p_group = pick(eg, e1 // EXPERTS_PER_GROUP) / jnp.sum(eg, axis=-1, keepdims=True)
        la = pick(lg, N_EXPERT_GROUPS + e1)
        lb = pick(lg, N_EXPERT_GROUPS + e2)
        mx = jnp.maximum(la, lb)
        ea = jnp.exp(la - mx)
        eb = jnp.exp(lb - mx)
        wa = ea / (ea + eb) * p_group
        wb = eb / (ea + eb) * p_group

        def expert(wg_ref, wu_ref, wd_ref):
            gt = jnp.dot(h, wg_ref[0].astype(BF16), preferred_element_type=F32)
            up = jnp.dot(h, wu_ref[0].astype(BF16), preferred_element_type=F32)
            act = (jax.nn.silu(gt) * up).astype(BF16)
            return jnp.dot(act, wd_ref[0].astype(BF16), preferred_element_type=F32)

        y = x + (wa * expert(wg1_ref, wu1_ref, wd1_ref) + wb * expert(wg2_ref, wu2_ref, wd2_ref))
        if final:
            y = _rms(y, nfin_ref[...])
        obuf[slot] = y
        for_rows(nv_ref[k], lambda r: scatter_copy(k, slot, r).start())

    @pl.when(k == n_tiles - 1)
    def _():
        for_rows(nv_ref[k], lambda r: scatter_copy(k, slot, r).wait())
        for_rows(nv_ref[jnp.maximum(k - 1, 0)], lambda r: scatter_copy(k - 1, 1 - slot, r).wait())


def _moe(x2, e1, e2, nv, src, nf_g, wrt, brt_row, w_gate, w_up, w_down, nfin_g, final):
    t = x2.shape[0]
    tile = MOE_TILE
    n_tiles = e1.shape[0]
    full = lambda a: pl.BlockSpec(a.shape, lambda k, *_: (0,) * a.ndim)
    wspec1 = lambda a: pl.BlockSpec((1,) + a.shape[1:], lambda k, e1r, e2r, nvr, srcr: (e1r[k], 0, 0))
    wspec2 = lambda a: pl.BlockSpec((1,) + a.shape[1:], lambda k, e1r, e2r, nvr, srcr: (e2r[k], 0, 0))
    anyspec = pl.BlockSpec(memory_space=pl.ANY)
    grid_spec = pltpu.PrefetchScalarGridSpec(
        num_scalar_prefetch=4,
        grid=(n_tiles,),
        in_specs=[anyspec, full(nf_g), full(wrt), full(brt_row),
                  wspec1(w_gate), wspec1(w_up), wspec1(w_down),
                  wspec2(w_gate), wspec2(w_up), wspec2(w_down), full(nfin_g)],
        out_specs=anyspec,
        scratch_shapes=[pltpu.VMEM((2, tile, D_MODEL), F32), pltpu.VMEM((2, tile, D_MODEL), F32),
                        pltpu.SemaphoreType.DMA((2,)), pltpu.SemaphoreType.DMA((2,))],
    )
    return pl.pallas_call(
        functools.partial(_moe_kernel, tile=tile, n_tiles=n_tiles, final=final),
        grid_spec=grid_spec,
        out_shape=jax.ShapeDtypeStruct((t, D_MODEL), F32),
        compiler_params=pltpu.CompilerParams(dimension_semantics=("arbitrary",), vmem_limit_bytes=VMEM_LIMIT),
        name="moe_experts",
    )(e1, e2, nv, src, x2, nf_g, wrt, brt_row, w_gate, w_up, w_down, w_gate, w_up, w_down, nfin_g)


def _route_meta(cls, rank, counts, t, tile, n_tiles):
    tiles_per = (counts + tile - 1) // tile
    tile_end = jnp.cumsum(tiles_per)
    tile_start = tile_end - tiles_per
    pos = tile_start[cls] * tile + rank
    src = jnp.zeros((n_tiles * tile,), jnp.int32).at[pos].set(jnp.arange(t, dtype=jnp.int32), unique_indices=True)
    kk = jnp.arange(n_tiles, dtype=jnp.int32)
    tcls = jnp.sum((kk[:, None] >= tile_end[None, :]).astype(jnp.int32), axis=1)
    valid = tcls < N_CLASSES
    last_cls = tcls[jnp.maximum(tile_end[-1] - 1, 0)]
    tc = jnp.where(valid, tcls, last_cls)
    nv = jnp.where(valid, jnp.clip(counts[tc] - (kk - tile_start[tc]) * tile, 0, tile), 0)
    grp = tc // 6
    e1 = grp * EXPERTS_PER_GROUP + jnp.asarray(PAIR_LO, jnp.int32)[tc % 6]
    e2 = grp * EXPERTS_PER_GROUP + jnp.asarray(PAIR_HI, jnp.int32)[tc % 6]
    return e1.astype(jnp.int32), e2.astype(jnp.int32), nv.astype(jnp.int32), src


def _rotary_lane_tables(positions):
    inv_freq = ROPE_THETA ** (-jnp.arange(0, ROPE_DIM, 2, dtype=F32) / ROPE_DIM)
    ang = positions.astype(F32).reshape(-1, 1) * inv_freq[None, :]
    cos, sin = jnp.cos(ang), jnp.sin(ang)
    half = ROPE_DIM // 2
    ones = jnp.ones((ang.shape[0], DIFF_QK_DIM - ROPE_DIM), F32)
    zeros = jnp.zeros((ang.shape[0], DIFF_QK_DIM - ROPE_DIM), F32)
    zh = jnp.zeros((ang.shape[0], half), F32)
    c = jnp.concatenate([cos, cos, ones], axis=1)
    s1 = jnp.concatenate([-sin, zh, zeros], axis=1)
    s2 = jnp.concatenate([zh, sin, zeros], axis=1)
    return tuple(jnp.concatenate([a, a], axis=1) for a in (c, s1, s2))


def kernel(x, mem, positions, norm_mix, w_in, lam_q1, lam_k1, lam_q2, lam_k2, subln_a, norm_b_out, sgu_norm, w_spatial, b_spatial, norm_c_out, w_out, norm_cross, norm_mem, w_cq, w_ck, w_cv, w_co, norm_ffn, w_group, b_group, w_router, b_router, w_gate, w_up, w_down, norm_final):
    batch, seq, d = x.shape
    depth = w_in.shape[0]
    t = batch * seq
    n_tiles = t // MOE_TILE + N_CLASSES
    x2 = x.reshape(t, d)
    rc, rs1, rs2 = _rotary_lane_tables(positions)
    kmem, vmem = _memkv(mem, norm_mem, w_ck, w_cv)
    row = lambda a: a.reshape(1, -1)
    for l in range(depth):
        lam_init = 0.8 - 0.6 * math.exp(-0.3 * l)
        aq, ak, av, bq, bk, bv, cu, cv = _inproj(x2, row(norm_mix[l]), w_in[l].astype(BF16), rc, rs1, rs2)
        lam_p = jnp.stack([lam_q1[l], lam_k1[l], lam_q2[l], lam_k2[l]])
        ya = _diff_attention(aq, ak, av, lam_p, row(subln_a[l]), lam_init, batch, seq)
        yb = _stick_breaking(bq, bk, bv, row(norm_b_out[l]), batch, seq)
        b_full = jnp.repeat(b_spatial[l].T, 64, axis=1)
        x2 = _mixout(x2, ya, yb, cu, cv, row(sgu_norm[l]), w_spatial[l], b_full, row(norm_c_out[l]),
                     w_out[l].astype(BF16))
        wrt = jnp.concatenate([w_group[l], w_router[l],
                               jnp.zeros((d, ROUTER_LANES - N_EXPERT_GROUPS - N_EXPERTS), F32)], axis=1)
        brt = jnp.concatenate([b_group[l], b_router[l],
                               jnp.zeros((ROUTER_LANES - N_EXPERT_GROUPS - N_EXPERTS,), F32)])
        x2, cls, rank, cnt = _cross_router(x2, row(norm_cross[l]), w_cq[l].astype(BF16), kmem[l], vmem[l],
                                           w_co[l].astype(BF16), row(norm_ffn[l]), wrt.T, brt.reshape(-1, 1), seq)
        e1, e2, nv, src = _route_meta(cls.reshape(t), rank.reshape(t), cnt[:N_CLASSES, 0], t, MOE_TILE, n_tiles)
        x2 = _moe(x2, e1, e2, nv, src, row(norm_ffn[l]), wrt, brt.reshape(1, -1), w_gate[l], w_up[l], w_down[l],
                  row(norm_final), final=(l == depth - 1))
    return x2.reshape(batch, seq, d)
```

```python
import functools
import math

import jax
import jax.numpy as jnp
from jax import lax
from jax.experimental import pallas as pl
from jax.experimental.pallas import tpu as pltpu

F32 = jnp.float32
BF16 = jnp.bfloat16
EPS = 1e-6

D_MODEL = 1024
N_HEADS_A = 4
DIFF_QK_DIM = 64
N_HEADS_B = 4
SB_DIM = 64
N_GROUPS_C = 4
CHUNK = 128
WIDTH_A = 512
WIDTH_B = 256
WIDTH_C = 256
D_IN = 2816
ROPE_THETA = 500000.0
ROPE_DIM = 16
MEM_LEN = 256
CROSS_WIDTH = 256
N_EXPERT_GROUPS = 4
EXPERTS_PER_GROUP = 4
N_EXPERTS = 16
D_EXPERT = 512
N_CLASSES = 24
CLASS_ROWS = 32
ROUTER_LANES = 128
PAIR_LO = (0, 0, 0, 1, 1, 2)
PAIR_HI = (1, 2, 3, 2, 3, 3)

ROW_TILE = 512
ATT_TILE = 256
ATT_CHUNK = 256
MOE_TILE = 256
VMEM_LIMIT = 56 * 1024 * 1024


def _rms(x, g):
    return x * lax.rsqrt(jnp.mean(x * x, axis=-1, keepdims=True) + EPS) * g


def _group_rms64(x, g):
    m, w = x.shape
    grp = lax.broadcasted_iota(jnp.int32, (m, w), 1) // 64
    sq = x * x
    scale = jnp.zeros_like(x)
    for gi in range(w // 64):
        sel = grp == gi
        ss = jnp.sum(jnp.where(sel, sq, 0.0), axis=-1, keepdims=True)
        scale = jnp.where(sel, lax.rsqrt(ss * (1.0 / 64.0) + EPS), scale)
    return x * scale * g


def _nt_dot(a, b, **kw):
    return lax.dot_general(a, b, (((1,), (1,)), ((), ())), preferred_element_type=F32, **kw)


def _inproj_kernel(x_ref, g_ref, w_ref, wvt_ref, c_ref, s1_ref, s2_ref,
                   aq_ref, ak_ref, avt_ref, bq_ref, bk_ref, bvt_ref, cu_ref, cv_ref, *, tm):
    h = _rms(x_ref[...], g_ref[...]).astype(BF16)

    def proj(lo, hi):
        return jnp.dot(h, w_ref[:, lo:hi], preferred_element_type=F32)

    c, s1, s2 = c_ref[...], s1_ref[...], s2_ref[...]

    def rot(t):
        outs = []
        for hh in range(4):
            th = t[:, hh * 128:(hh + 1) * 128]
            outs.append(th * c + pltpu.roll(th, 120, 1) * s1 + pltpu.roll(th, 8, 1) * s2)
        return jnp.concatenate(outs, axis=1)

    aq_ref[...] = (rot(proj(0, 512)) * 0.125).astype(BF16)
    ak_ref[...] = rot(proj(512, 1024)).astype(BF16)
    bq_ref[...] = (proj(1536, 1792) * 0.125).astype(BF16)
    bk_ref[...] = proj(1792, 2048).astype(BF16)
    cu_ref[...] = jax.nn.gelu(proj(2304, 2560))
    cv_ref[...] = jax.nn.gelu(proj(2560, 2816))
    vt = _nt_dot(wvt_ref[...], h).astype(BF16)
    for blk in range(tm // ATT_TILE):
        cols = slice(blk * ATT_TILE, (blk + 1) * ATT_TILE)
        avt_ref[blk] = vt[0:WIDTH_A, cols]
        bvt_ref[blk] = vt[WIDTH_A:, cols]


def _inproj(x2, g, w_bf, wvt_bf, rc, rs1, rs2):
    t = x2.shape[0]
    tm = ROW_TILE
    per = tm // ATT_TILE
    row = lambda w: pl.BlockSpec((tm, w), lambda i: (i, 0))
    full = lambda a: pl.BlockSpec(a.shape, lambda i: (0,) * a.ndim)
    slab = lambda w: pl.BlockSpec((per, w, ATT_TILE), lambda i: (i, 0, 0))
    sds = jax.ShapeDtypeStruct
    outs = [sds((t, 512), BF16), sds((t, 512), BF16), sds((t // ATT_TILE, WIDTH_A, ATT_TILE), BF16),
            sds((t, 256), BF16), sds((t, 256), BF16), sds((t // ATT_TILE, WIDTH_B, ATT_TILE), BF16),
            sds((t, 256), F32), sds((t, 256), F32)]
    return pl.pallas_call(
        functools.partial(_inproj_kernel, tm=tm),
        grid=(t // tm,),
        in_specs=[row(D_MODEL), full(g), full(w_bf), full(wvt_bf), row(128), row(128), row(128)],
        out_specs=[row(512), row(512), slab(WIDTH_A), row(256), row(256), slab(WIDTH_B), row(256), row(256)],
        out_shape=outs,
        compiler_params=pltpu.CompilerParams(dimension_semantics=("parallel",), vmem_limit_bytes=VMEM_LIMIT),
        name="inproj",
    )(x2, g, w_bf, wvt_bf, rc, rs1, rs2)


def _stack_queries(q_ref, qs_ref, tq):
    q = q_ref[...]
    lane = lax.broadcasted_iota(jnp.int32, (tq, 128), 1)
    zero = jnp.zeros_like(q)
    qs_ref[0:tq, :] = jnp.where(lane < 64, q, zero)
    qs_ref[tq:, :] = jnp.where(lane >= 64, q, zero)


def _key_le_query(tq, cw, c, strict):
    key = lax.broadcasted_iota(jnp.int32, (tq, cw), 0)
    qry = lax.broadcasted_iota(jnp.int32, (tq, cw), 1) + (c * cw) % tq
    return key < qry if strict else key <= qry


def _diff_kernel(lam_ref, g_ref, q_ref, k_ref, vt_ref, o_ref, qs_ref, m_ref, l_ref, acc_ref,
                 s_buf, p_buf, a_buf, *, tq, cw, lam_init):
    i = pl.program_id(2)
    chunks = [slice(c * cw, (c + 1) * cw) for c in range(2 * tq // cw)]
    _stack_queries(q_ref, qs_ref, tq)
    m_ref[...] = jnp.full_like(m_ref, -jnp.inf)
    l_ref[...] = jnp.zeros_like(l_ref)
    acc_ref[...] = jnp.zeros_like(acc_ref)
    p_buf[...] = jnp.zeros_like(p_buf)
    a_buf[...] = jnp.ones_like(a_buf)

    def scores(j):
        k = k_ref[pl.ds(pl.multiple_of(j * tq, tq), tq), :]
        return [_nt_dot(k, qs_ref[cs, :]) for cs in chunks]

    def hand_off(s_new):
        for c, cs in enumerate(chunks):
            s_buf[:, cs] = s_new[c]

    def softmax(masked):
        for c, cs in enumerate(chunks):
            sc = s_buf[:, cs]
            if masked:
                sc = jnp.where(_key_le_query(tq, cw, c, False), sc, -jnp.inf)
            m_old = m_ref[:, cs]
            m_new = jnp.maximum(m_old, jnp.max(sc, axis=0, keepdims=True))
            alpha = jnp.exp(m_old - m_new)
            p = jnp.exp(sc - m_new)
            l_ref[:, cs] = alpha * l_ref[:, cs] + jnp.sum(p, axis=0, keepdims=True)
            m_ref[:, cs] = m_new
            p_buf[:, cs] = p.astype(BF16)
            a_buf[:, cs] = alpha

    def values_matmul(j):
        vt = vt_ref[jnp.maximum(j, 0)]
        return [(a_buf[:, cs], jnp.dot(vt, p_buf[:, cs], preferred_element_type=F32)) for cs in chunks]

    def accumulate(scaled):
        for cs, (alpha, pv) in zip(chunks, scaled):
            acc_ref[:, cs] = alpha * acc_ref[:, cs] + pv

    def values(j):
        accumulate(values_matmul(j))

    hand_off(scores(0))

    def body(t, carry):
        s_new = scores(t + 1)
        scaled = values_matmul(t - 1)
        softmax(False)
        accumulate(scaled)
        hand_off(s_new)
        return carry

    lax.fori_loop(0, i, body, 0)
    values(i - 1)
    softmax(True)
    values(i)

    lp = lam_ref[...]
    lam = (jnp.exp(jnp.sum(lp[0:1] * lp[1:2], axis=-1, keepdims=True))
           - jnp.exp(jnp.sum(lp[2:3] * lp[3:4], axis=-1, keepdims=True)) + lam_init)
    o = acc_ref[...] / l_ref[...]
    o = o[:, 0:tq] - lam * o[:, tq:]
    o = o * lax.rsqrt(jnp.mean(o * o, axis=0, keepdims=True) + EPS)
    o_ref[...] = (o.T * g_ref[...] * (1.0 - lam_init)).astype(o_ref.dtype)


def _diff_attention(aq, ak, avt, lam_p, subln_g, lam_init, batch, seq):
    tq = ATT_TILE
    nq = seq // tq
    kern = functools.partial(_diff_kernel, tq=tq, cw=ATT_CHUNK, lam_init=lam_init)
    qspec = pl.BlockSpec((tq, 128), lambda b, h, i: (b * nq + i, h))
    kspec = pl.BlockSpec((seq, 128), lambda b, h, i: (b, h))
    vspec = pl.BlockSpec((nq, 128, tq), lambda b, h, i: (b, h, 0))
    return pl.pallas_call(
        kern,
        grid=(batch, N_HEADS_A, nq),
        in_specs=[pl.BlockSpec(lam_p.shape, lambda b, h, i: (0, 0)),
                  pl.BlockSpec(subln_g.shape, lambda b, h, i: (0, 0)),
                  qspec, kspec, vspec],
        out_specs=qspec,
        out_shape=jax.ShapeDtypeStruct(aq.shape, BF16),
        scratch_shapes=[pltpu.VMEM((2 * tq, 128), BF16), pltpu.VMEM((1, 2 * tq), F32),
                        pltpu.VMEM((1, 2 * tq), F32), pltpu.VMEM((128, 2 * tq), F32),
                        pltpu.VMEM((tq, 2 * tq), F32), pltpu.VMEM((tq, 2 * tq), BF16),
                        pltpu.VMEM((1, 2 * tq), F32)],
        compiler_params=pltpu.CompilerParams(dimension_semantics=("parallel", "parallel", "arbitrary"),
                                             vmem_limit_bytes=VMEM_LIMIT),
        name="diff_attn",
    )(lam_p, subln_g, aq, ak, avt)


def _sb_kernel(g_ref, u_ref, q_ref, k_ref, vt_ref, o_ref, qs_ref, c_ref, acc_ref,
               z_buf, hi_buf, lo_buf, r_buf, w_buf, *, tq, cw):
    i = pl.program_id(2)
    chunks = [slice(c * cw, (c + 1) * cw) for c in range(2 * tq // cw)]
    _stack_queries(q_ref, qs_ref, tq)
    c_ref[...] = jnp.zeros_like(c_ref)
    acc_ref[...] = jnp.zeros_like(acc_ref)
    w_buf[...] = jnp.zeros_like(w_buf)

    def scores(n):
        k = k_ref[pl.ds(pl.multiple_of((i - n) * tq, tq), tq), :]
        return [_nt_dot(k, qs_ref[cs, :]) for cs in chunks]

    def hand_off(z_new):
        for c, cs in enumerate(chunks):
            z_buf[:, cs] = z_new[c]

    def breaks(masked):
        for c, cs in enumerate(chunks):
            z = z_buf[:, cs]
            sp = jnp.maximum(z, 0.0) + jnp.log(1.0 + jnp.exp(-jnp.abs(z)))
            prior = c_ref[:, cs]
            rest = z - sp - prior
            if masked:
                strict = _key_le_query(tq, cw, c, True)
                sp = jnp.where(strict, sp, 0.0)
                rest = jnp.where(strict, rest, -jnp.inf)
            hi = sp.astype(BF16)
            hi_buf[:, cs] = hi
            lo_buf[:, cs] = (sp - hi.astype(F32)).astype(BF16)
            r_buf[:, cs] = rest
            c_ref[:, cs] = prior + jnp.sum(sp, axis=0, keepdims=True)

    def weights():
        u = u_ref[...]
        for cs in chunks:
            cum = (jnp.dot(u, hi_buf[:, cs], preferred_element_type=F32)
                   + jnp.dot(u, lo_buf[:, cs], preferred_element_type=F32))
            w_buf[:, cs] = jnp.exp(r_buf[:, cs] - cum).astype(BF16)

    def values(n):
        vt = vt_ref[i - jnp.maximum(n, 0)]
        for cs in chunks:
            acc_ref[:, cs] += jnp.dot(vt, w_buf[:, cs], preferred_element_type=F32)

    hand_off(scores(0))
    breaks(True)
    hand_off(scores(jnp.minimum(1, i)))

    def body(n, carry):
        values(n - 2)
        weights()
        breaks(False)
        hand_off(scores(jnp.minimum(n + 1, i)))
        return carry

    lax.fori_loop(1, i + 1, body, 0)
    values(i - 1)
    weights()
    values(i)

    def head_norm(o):
        return o * lax.rsqrt(jnp.mean(o * o, axis=0, keepdims=True) + EPS)

    o = jnp.concatenate([head_norm(acc_ref[0:64, 0:tq]), head_norm(acc_ref[64:128, tq:])], axis=0)
    o_ref[...] = (o.T * g_ref[...]).astype(o_ref.dtype)


def _stick_breaking(bq, bk, bvt, norm_g, batch, seq):
    tq = ATT_TILE
    nq = seq // tq
    u = (lax.broadcasted_iota(jnp.int32, (tq, tq), 1) > lax.broadcasted_iota(jnp.int32, (tq, tq), 0)).astype(BF16)
    kern = functools.partial(_sb_kernel, tq=tq, cw=ATT_CHUNK)
    qspec = pl.BlockSpec((tq, 128), lambda b, h, i: (b * nq + i, h))
    kspec = pl.BlockSpec((seq, 128), lambda b, h, i: (b, h))
    vspec = pl.BlockSpec((nq, 128, tq), lambda b, h, i: (b, h, 0))
    return pl.pallas_call(
        kern,
        grid=(batch, N_HEADS_B // 2, nq),
        in_specs=[pl.BlockSpec((1, 128), lambda b, h, i: (0, h)),
                  pl.BlockSpec((tq, tq), lambda b, h, i: (0, 0)),
                  qspec, kspec, vspec],
        out_specs=qspec,
        out_shape=jax.ShapeDtypeStruct(bq.shape, BF16),
        scratch_shapes=[pltpu.VMEM((2 * tq, 128), BF16), pltpu.VMEM((1, 2 * tq), F32),
                        pltpu.VMEM((128, 2 * tq), F32),
                        pltpu.VMEM((tq, 2 * tq), F32), pltpu.VMEM((tq, 2 * tq), BF16),
                        pltpu.VMEM((tq, 2 * tq), BF16), pltpu.VMEM((tq, 2 * tq), F32),
                        pltpu.VMEM((tq, 2 * tq), BF16)],
        compiler_params=pltpu.CompilerParams(dimension_semantics=("parallel", "parallel", "arbitrary"),
                                             vmem_limit_bytes=VMEM_LIMIT),
        name="stick_breaking",
    )(norm_g, u, bq, bk, bvt)


def _mixout_kernel(x_ref, ya_ref, yb_ref, cu_ref, cv_ref, sg_ref, ws_ref, bs_ref, nc_ref, wo_ref, o_ref, *, tm):
    vn = _group_rms64(cv_ref[...], sg_ref[...]).astype(BF16)
    tril = lax.broadcasted_iota(jnp.int32, (CHUNK, CHUNK), 0) >= lax.broadcasted_iota(jnp.int32, (CHUNK, CHUNK), 1)
    ws = [jnp.where(tril, ws_ref[g], 0.0).astype(BF16) for g in range(N_GROUPS_C)]
    grp = lax.broadcasted_iota(jnp.int32, (CHUNK, WIDTH_C), 1) // 64
    bias = bs_ref[...]
    chunks = []
    for n in range(tm // CHUNK):
        vc = vn[n * CHUNK:(n + 1) * CHUNK, :]
        s = jnp.zeros((CHUNK, WIDTH_C), F32)
        for g in range(N_GROUPS_C):
            s = jnp.where(grp == g, jnp.dot(ws[g], vc, preferred_element_type=F32), s)
        chunks.append(s + bias)
    yc = cu_ref[...] * jnp.concatenate(chunks, axis=0)
    yc = _group_rms64(yc, nc_ref[...]).astype(BF16)
    o_ref[...] = (x_ref[...]
                  + jnp.dot(ya_ref[...], wo_ref[0:512, :], preferred_element_type=F32)
                  + jnp.dot(yb_ref[...], wo_ref[512:768, :], preferred_element_type=F32)
                  + jnp.dot(yc, wo_ref[768:1024, :], preferred_element_type=F32))


def _mixout(x2, ya, yb, cu, cv, sgu_g, w_s, b_full, nc_g, wo_bf):
    t = x2.shape[0]
    tm = ROW_TILE
    row = lambda w: pl.BlockSpec((tm, w), lambda i: (i, 0))
    full = lambda a: pl.BlockSpec(a.shape, lambda i: (0,) * a.ndim)
    return pl.pallas_call(
        functools.partial(_mixout_kernel, tm=tm),
        grid=(t // tm,),
        in_specs=[row(D_MODEL), row(512), row(256), row(256), row(256),
                  full(sgu_g), full(w_s), full(b_full), full(nc_g), full(wo_bf)],
        out_specs=row(D_MODEL),
        out_shape=jax.ShapeDtypeStruct(x2.shape, F32),
        compiler_params=pltpu.CompilerParams(dimension_semantics=("parallel",), vmem_limit_bytes=VMEM_LIMIT),
        name="sgu_outproj",
    )(x2, ya, yb, cu, cv, sgu_g, w_s, b_full, nc_g, wo_bf)


def _memkv_kernel(m_ref, g_ref, wk_ref, wv_ref, k_ref, v_ref):
    h = _rms(m_ref[0], g_ref[0]).astype(BF16)
    k_ref[0, 0] = jnp.dot(h, wk_ref[0].astype(BF16), preferred_element_type=F32).astype(BF16)
    v_ref[0, 0] = jnp.dot(h, wv_ref[0].astype(BF16), preferred_element_type=F32).astype(BF16)


def _memkv(mem, norm_mem, w_ck, w_cv):
    depth = w_ck.shape[0]
    batch = mem.shape[0]
    out = jax.ShapeDtypeStruct((depth, batch, MEM_LEN, CROSS_WIDTH), BF16)
    wspec = pl.BlockSpec((1, D_MODEL, CROSS_WIDTH), lambda l, b: (l, 0, 0))
    ospec = pl.BlockSpec((1, 1, MEM_LEN, CROSS_WIDTH), lambda l, b: (l, b, 0, 0))
    return pl.pallas_call(
        _memkv_kernel,
        grid=(depth, batch),
        in_specs=[pl.BlockSpec((1, MEM_LEN, D_MODEL), lambda l, b: (b, 0, 0)),
                  pl.BlockSpec((1, 1, D_MODEL), lambda l, b: (l, 0, 0)), wspec, wspec],
        out_specs=[ospec, ospec],
        out_shape=[out, out],
        compiler_params=pltpu.CompilerParams(dimension_semantics=("parallel", "parallel"),
                                             vmem_limit_bytes=VMEM_LIMIT),
        name="mem_kv",
    )(mem, norm_mem.reshape(depth, 1, D_MODEL), w_ck, w_cv)


def _cross_kernel(x_ref, nc_ref, wq_ref, km_ref, vm_ref, wo_ref, nf_ref, wrt_ref, brt_ref, u_ref,
                  o_ref, cls_ref, rank_ref, cnt_ref, carry_ref, *, tm):
    @pl.when(pl.program_id(0) == 0)
    def _():
        carry_ref[...] = jnp.zeros_like(carry_ref)

    x = x_ref[...]
    h = _rms(x, nc_ref[...]).astype(BF16)
    q = (jnp.dot(h, wq_ref[...], preferred_element_type=F32) * 0.125).astype(BF16)
    km = km_ref[0]
    vm = vm_ref[0]
    qgrp = lax.broadcasted_iota(jnp.int32, (tm, CROSS_WIDTH), 1) // 64
    vgrp = lax.broadcasted_iota(jnp.int32, (MEM_LEN, CROSS_WIDTH), 1) // 64
    o = jnp.zeros((tm, CROSS_WIDTH), F32)
    for hh in range(4):
        s = _nt_dot(jnp.where(qgrp == hh, q, jnp.zeros_like(q)), km)
        p = jnp.exp(s - jnp.max(s, axis=-1, keepdims=True))
        p = p / jnp.sum(p, axis=-1, keepdims=True)
        o = o + jnp.dot(p.astype(BF16), jnp.where(vgrp == hh, vm, jnp.zeros_like(vm)),
                        preferred_element_type=F32)
    x2 = x + jnp.dot(o.astype(BF16), wo_ref[...], preferred_element_type=F32)
    o_ref[...] = x2

    hf = _rms(x2, nf_ref[...])
    lt = _nt_dot(wrt_ref[...], hf, precision=lax.Precision.HIGHEST) + brt_ref[...]
    gl = [lt[r:r + 1, :] for r in range(4)]
    el = [lt[4 + r:5 + r, :] for r in range(16)]

    def first_argmax(vals):
        mx = functools.reduce(jnp.maximum, vals)
        idx = jnp.full(vals[0].shape, len(vals) - 1, jnp.int32)
        for r in range(len(vals) - 2, -1, -1):
            idx = jnp.where(vals[r] == mx, r, idx)
        return idx

    g = first_argmax(gl)
    es = [jnp.where(g == 0, el[r], jnp.where(g == 1, el[4 + r], jnp.where(g == 2, el[8 + r], el[12 + r])))
          for r in range(4)]
    i1 = first_argmax(es)
    i2 = first_argmax([jnp.where(i1 == r, -jnp.inf, es[r]) for r in range(4)])
    a = jnp.minimum(i1, i2)
    b = jnp.maximum(i1, i2)
    cls = g * 6 + jnp.where(a == 0, b - 1, jnp.where(a == 1, b + 1, 5))

    onehot = lax.broadcasted_iota(jnp.int32, (CLASS_ROWS, tm), 0) == cls
    oh = jnp.where(onehot, 1.0, 0.0)
    before = jnp.dot(oh.astype(BF16), u_ref[...], preferred_element_type=F32) + carry_ref[...]
    rank = jnp.sum(jnp.where(onehot, before, 0.0), axis=0, keepdims=True)
    carry_ref[...] += jnp.sum(oh, axis=1, keepdims=True)
    cls_ref[0] = cls
    rank_ref[0] = rank.astype(jnp.int32)
    cnt_ref[...] = jnp.broadcast_to(carry_ref[...], cnt_ref.shape).astype(jnp.int32)


def _cross_router(x2, nc_g, wq_bf, kmem, vmem, wo_bf, nf_g, wrt_t, brt_col, seq):
    t = x2.shape[0]
    tm = ROW_TILE
    nt = t // tm
    per_b = seq // tm
    u = (lax.broadcasted_iota(jnp.int32, (tm, tm), 0) < lax.broadcasted_iota(jnp.int32, (tm, tm), 1)).astype(BF16)
    row = pl.BlockSpec((tm, D_MODEL), lambda i: (i, 0))
    full = lambda a: pl.BlockSpec(a.shape, lambda i: (0,) * a.ndim)
    mspec = pl.BlockSpec((1, MEM_LEN, CROSS_WIDTH), lambda i: (i // per_b, 0, 0))
    tokspec = pl.BlockSpec((1, 1, tm), lambda i: (i, 0, 0))
    return pl.pallas_call(
        functools.partial(_cross_kernel, tm=tm),
        grid=(nt,),
        in_specs=[row, full(nc_g), full(wq_bf), mspec, mspec, full(wo_bf), full(nf_g), full(wrt_t),
                  full(brt_col), full(u)],
        out_specs=[row, tokspec, tokspec, pl.BlockSpec((CLASS_ROWS, 128), lambda i: (0, 0))],
        out_shape=[jax.ShapeDtypeStruct(x2.shape, F32), jax.ShapeDtypeStruct((nt, 1, tm), jnp.int32),
                   jax.ShapeDtypeStruct((nt, 1, tm), jnp.int32),
                   jax.ShapeDtypeStruct((CLASS_ROWS, 128), jnp.int32)],
        scratch_shapes=[pltpu.VMEM((CLASS_ROWS, 1), F32)],
        compiler_params=pltpu.CompilerParams(dimension_semantics=("arbitrary",), vmem_limit_bytes=VMEM_LIMIT),
        name="cross_router",
    )(x2, nc_g, wq_bf, kmem, vmem, wo_bf, nf_g, wrt_t, brt_col, u)


def _moe_kernel(e1_ref, e2_ref, nv_ref, src_ref,
                x_hbm, nf_ref, wrt_ref, brt_ref, wg1_ref, wu1_ref, wd1_ref, wg2_ref, wu2_ref, wd2_ref,
                nfin_ref, out_hbm, xbuf, obuf, wgc, wuc, wdc, gsem, ssem, *, tile, n_tiles, final):
    k = pl.program_id(0)
    slot = k % 2
    prev = jnp.maximum(k - 1, 0)

    def gather_row(kk, sl, r):
        tok = src_ref[kk * tile + r]
        return pltpu.make_async_copy(x_hbm.at[pl.ds(tok, 1), :], xbuf.at[sl, pl.ds(r, 1), :], gsem.at[sl])

    def scatter_row(kk, sl, r):
        tok = src_ref[kk * tile + r]
        return pltpu.make_async_copy(obuf.at[sl, pl.ds(r, 1), :], out_hbm.at[pl.ds(tok, 1), :], ssem.at[sl])

    def start_gather(kk, sl):
        def body(r, c):
            gather_row(kk, sl, r).start()
            return c
        lax.fori_loop(0, tile, body, 0, unroll=8)

    def wait_gather(sl):
        pltpu.make_async_copy(x_hbm.at[pl.ds(0, tile), :], xbuf.at[sl], gsem.at[sl]).wait()

    def wait_scatter(n, sl):
        n8 = pl.multiple_of((n // 8) * 8, 8)

        @pl.when(n8 > 0)
        def _():
            pltpu.make_async_copy(obuf.at[sl, pl.ds(0, n8), :], out_hbm.at[pl.ds(0, n8), :], ssem.at[sl]).wait()

        def body(r, c):
            pltpu.make_async_copy(obuf.at[sl, pl.ds(0, 1), :], out_hbm.at[pl.ds(0, 1), :], ssem.at[sl]).wait()
            return c
        lax.fori_loop(n8, n, body, 0)

    @pl.when(jnp.logical_and(k == 0, nv_ref[0] > 0))
    def _():
        start_gather(0, 0)

    @pl.when(k >= 2)
    def _():
        wait_scatter(nv_ref[jnp.maximum(k - 2, 0)], slot)

    nxt = jnp.minimum(k + 1, n_tiles - 1)

    @pl.when(jnp.logical_and(k + 1 < n_tiles, nv_ref[nxt] > 0))
    def _():
        start_gather(k + 1, 1 - slot)

    @pl.when(nv_ref[k] > 0)
    def _():
        @pl.when(jnp.logical_or(k == 0, e1_ref[k] != e1_ref[prev]))
        def _():
            wgc[0] = wg1_ref[0].astype(BF16)
            wuc[0] = wu1_ref[0].astype(BF16)
            wdc[0] = wd1_ref[0].astype(BF16)

        @pl.when(jnp.logical_or(k == 0, e2_ref[k] != e2_ref[prev]))
        def _():
            wgc[1] = wg2_ref[0].astype(BF16)
            wuc[1] = wu2_ref[0].astype(BF16)
            wdc[1] = wd2_ref[0].astype(BF16)

        wait_gather(slot)
        x = xbuf[slot]
        hf = _rms(x, nf_ref[...])
        h = hf.astype(BF16)
        lg = jnp.dot(hf, wrt_ref[...], preferred_element_type=F32, precision=lax.Precision.HIGHEST) + brt_ref[...]
        lane = lax.broadcasted_iota(jnp.int32, lg.shape, 1)
        e1 = e1_ref[k]
        e2 = e2_ref[k]
        glm = jnp.where(lane < N_EXPERT_GROUPS, lg, -jnp.inf)
        eg = jnp.exp(glm - jnp.max(glm, axis=-1, keepdims=True))
        pick = lambda a, idx: jnp.sum(jnp.where(lane == idx, a, 0.0), axis=-1, keepdims=True)
        p_group = pick(eg, e1 // EXPERTS_PER_GROUP) / jnp.sum(eg, axis=-1, keepdims=True)
        la = pick(lg, N_EXPERT_GROUPS + e1)
        lb = pick(lg, N_EXPERT_GROUPS + e2)
        mx = jnp.maximum(la, lb)
        ea = jnp.exp(la - mx)
        eb = jnp.exp(lb - mx)
        wa = ea / (ea + eb) * p_group
        wb = eb / (ea + eb) * p_group

        def expert(s):
            gt = jnp.dot(h, wgc[s], preferred_element_type=F32)
            up = jnp.dot(h, wuc[s], preferred_element_type=F32)
            act = (jax.nn.silu(gt) * up).astype(BF16)
            return jnp.dot(act, wdc[s], preferred_element_type=F32)

        y = x + (wa * expert(0) + wb * expert(1))
        if final:
            y = _rms(y, nfin_ref[...])
        obuf[slot] = y

        n_rows = nv_ref[k]

        def body8(g, c):
            for r in range(8):
                scatter_row(k, slot, g * 8 + r).start()
            return c
        lax.fori_loop(0, n_rows // 8, body8, 0)

        def body(r, c):
            scatter_row(k, slot, r).start()
            return c
        lax.fori_loop((n_rows // 8) * 8, n_rows, body, 0)

    @pl.when(k == n_tiles - 1)
    def _():
        wait_scatter(nv_ref[k], slot)
        wait_scatter(nv_ref[prev], 1 - slot)


def _moe(x2, e1, e2, nv, src, nf_g, wrt, brt_row, w_gate, w_up, w_down, nfin_g, final):
    t = x2.shape[0]
    tile = MOE_TILE
    n_tiles = e1.shape[0]
    full = lambda a: pl.BlockSpec(a.shape, lambda k, *_: (0,) * a.ndim)
    wspec1 = lambda a: pl.BlockSpec((1,) + a.shape[1:], lambda k, e1r, e2r, nvr, srcr: (e1r[k], 0, 0))
    wspec2 = lambda a: pl.BlockSpec((1,) + a.shape[1:], lambda k, e1r, e2r, nvr, srcr: (e2r[k], 0, 0))
    anyspec = pl.BlockSpec(memory_space=pl.ANY)
    grid_spec = pltpu.PrefetchScalarGridSpec(
        num_scalar_prefetch=4,
        grid=(n_tiles,),
        in_specs=[anyspec, full(nf_g), full(wrt), full(brt_row),
                  wspec1(w_gate), wspec1(w_up), wspec1(w_down),
                  wspec2(w_gate), wspec2(w_up), wspec2(w_down), full(nfin_g)],
        out_specs=anyspec,
        scratch_shapes=[pltpu.VMEM((2, tile, D_MODEL), F32), pltpu.VMEM((2, tile, D_MODEL), F32),
                        pltpu.VMEM((2, D_MODEL, D_EXPERT), BF16), pltpu.VMEM((2, D_MODEL, D_EXPERT), BF16),
                        pltpu.VMEM((2, D_EXPERT, D_MODEL), BF16),
                        pltpu.SemaphoreType.DMA((2,)), pltpu.SemaphoreType.DMA((2,))],
    )
    return pl.pallas_call(
        functools.partial(_moe_kernel, tile=tile, n_tiles=n_tiles, final=final),
        grid_spec=grid_spec,
        out_shape=jax.ShapeDtypeStruct((t, D_MODEL), F32),
        compiler_params=pltpu.CompilerParams(dimension_semantics=("arbitrary",), vmem_limit_bytes=VMEM_LIMIT),
        name="moe_experts",
    )(e1, e2, nv, src, x2, nf_g, wrt, brt_row, w_gate, w_up, w_down, w_gate, w_up, w_down, nfin_g)


def _route_meta(cls, rank, counts, t, tile, n_tiles):
    tiles_per = (counts + tile - 1) // tile
    tile_end = jnp.cumsum(tiles_per)
    tile_start = tile_end - tiles_per
    pos = tile_start[cls] * tile + rank
    src = jnp.zeros((n_tiles * tile,), jnp.int32).at[pos].set(jnp.arange(t, dtype=jnp.int32), unique_indices=True)
    kk = jnp.arange(n_tiles, dtype=jnp.int32)
    tcls = jnp.sum((kk[:, None] >= tile_end[None, :]).astype(jnp.int32), axis=1)
    valid = tcls < N_CLASSES
    last_cls = tcls[jnp.maximum(tile_end[-1] - 1, 0)]
    tc = jnp.where(valid, tcls, last_cls)
    nv = jnp.where(valid, jnp.clip(counts[tc] - (kk - tile_start[tc]) * tile, 0, tile), 0)
    grp = tc // 6
    e1 = grp * EXPERTS_PER_GROUP + jnp.asarray(PAIR_LO, jnp.int32)[tc % 6]
    e2 = grp * EXPERTS_PER_GROUP + jnp.asarray(PAIR_HI, jnp.int32)[tc % 6]
    return e1.astype(jnp.int32), e2.astype(jnp.int32), nv.astype(jnp.int32), src


def _rotary_lane_tables(positions):
    inv_freq = ROPE_THETA ** (-jnp.arange(0, ROPE_DIM, 2, dtype=F32) / ROPE_DIM)
    ang = positions.astype(F32).reshape(-1, 1) * inv_freq[None, :]
    cos, sin = jnp.cos(ang), jnp.sin(ang)
    half = ROPE_DIM // 2
    ones = jnp.ones((ang.shape[0], DIFF_QK_DIM - ROPE_DIM), F32)
    zeros = jnp.zeros((ang.shape[0], DIFF_QK_DIM - ROPE_DIM), F32)
    zh = jnp.zeros((ang.shape[0], half), F32)
    c = jnp.concatenate([cos, cos, ones], axis=1)
    s1 = jnp.concatenate([-sin, zh, zeros], axis=1)
    s2 = jnp.concatenate([zh, sin, zeros], axis=1)
    return tuple(jnp.concatenate([a, a], axis=1) for a in (c, s1, s2))


def kernel(x, mem, positions, norm_mix, w_in, lam_q1, lam_k1, lam_q2, lam_k2, subln_a, norm_b_out, sgu_norm, w_spatial, b_spatial, norm_c_out, w_out, norm_cross, norm_mem, w_cq, w_ck, w_cv, w_co, norm_ffn, w_group, b_group, w_router, b_router, w_gate, w_up, w_down, norm_final):
    batch, seq, d = x.shape
    depth = w_in.shape[0]
    t = batch * seq
    n_tiles = t // MOE_TILE + N_CLASSES
    x2 = x.reshape(t, d)
    rc, rs1, rs2 = _rotary_lane_tables(positions)
    kmem, vmem = _memkv(mem, norm_mem, w_ck, w_cv)
    row = lambda a: a.reshape(1, -1)
    for l in range(depth):
        lam_init = 0.8 - 0.6 * math.exp(-0.3 * l)
        w_vt = jnp.concatenate([w_in[l][:, 1024:1536], w_in[l][:, 2048:2304]], axis=1).T.astype(BF16)
        aq, ak, avt, bq, bk, bvt, cu, cv = _inproj(x2, row(norm_mix[l]), w_in[l].astype(BF16), w_vt, rc, rs1, rs2)
        lam_p = jnp.stack([lam_q1[l], lam_k1[l], lam_q2[l], lam_k2[l]])
        ya = _diff_attention(aq, ak, avt, lam_p, row(subln_a[l]), lam_init, batch, seq)
        yb = _stick_breaking(bq, bk, bvt, row(norm_b_out[l]), batch, seq)
        b_full = jnp.repeat(b_spatial[l].T, 64, axis=1)
        x2 = _mixout(x2, ya, yb, cu, cv, row(sgu_norm[l]), w_spatial[l], b_full, row(norm_c_out[l]),
                     w_out[l].astype(BF16))
        wrt = jnp.concatenate([w_group[l], w_router[l],
                               jnp.zeros((d, ROUTER_LANES - N_EXPERT_GROUPS - N_EXPERTS), F32)], axis=1)
        brt = jnp.concatenate([b_group[l], b_router[l],
                               jnp.zeros((ROUTER_LANES - N_EXPERT_GROUPS - N_EXPERTS,), F32)])
        x2, cls, rank, cnt = _cross_router(x2, row(norm_cross[l]), w_cq[l].astype(BF16), kmem[l], vmem[l],
                                           w_co[l].astype(BF16), row(norm_ffn[l]), wrt.T, brt.reshape(-1, 1), seq)
        e1, e2, nv, src = _route_meta(cls.reshape(t), rank.reshape(t), cnt[:N_CLASSES, 0], t, MOE_TILE, n_tiles)
        x2 = _moe(x2, e1, e2, nv, src, row(norm_ffn[l]), wrt, brt.reshape(1, -1), w_gate[l], w_up[l], w_down[l],
                  row(norm_final), final=(l == depth - 1))
    return x2.reshape(batch, seq, d)
```

```python
import functools
import math

import jax
import jax.numpy as jnp
from jax import lax
from jax.experimental import pallas as pl
from jax.experimental.pallas import tpu as pltpu

F32 = jnp.float32
BF16 = jnp.bfloat16
EPS = 1e-6

D_MODEL = 1024
N_HEADS_A = 4
DIFF_QK_DIM = 64
N_HEADS_B = 4
SB_DIM = 64
N_GROUPS_C = 4
CHUNK = 128
WIDTH_A = 512
WIDTH_B = 256
WIDTH_C = 256
D_IN = 2816
ROPE_THETA = 500000.0
ROPE_DIM = 16
MEM_LEN = 256
CROSS_WIDTH = 256
N_EXPERT_GROUPS = 4
EXPERTS_PER_GROUP = 4
N_EXPERTS = 16
D_EXPERT = 512
N_CLASSES = 24
CLASS_ROWS = 32
ROUTER_LANES = 128
PAIR_LO = (0, 0, 0, 1, 1, 2)
PAIR_HI = (1, 2, 3, 2, 3, 3)

ROW_TILE = 512
DIFF_TILE = 512
SB_TILE = 256
ATT_CHUNK = 256
SOFTMAX_SLAB = 128
LOG2E = 1.4426950408889634
MOE_TILE = 256
VMEM_LIMIT = 56 * 1024 * 1024


def _rms(x, g):
    return x * lax.rsqrt(jnp.mean(x * x, axis=-1, keepdims=True) + EPS) * g


def _group_rms64(x, g):
    m, w = x.shape
    grp = lax.broadcasted_iota(jnp.int32, (m, w), 1) // 64
    sq = x * x
    scale = jnp.zeros_like(x)
    for gi in range(w // 64):
        sel = grp == gi
        ss = jnp.sum(jnp.where(sel, sq, 0.0), axis=-1, keepdims=True)
        scale = jnp.where(sel, lax.rsqrt(ss * (1.0 / 64.0) + EPS), scale)
    return x * scale * g


def _nt_dot(a, b, **kw):
    return lax.dot_general(a, b, (((1,), (1,)), ((), ())), preferred_element_type=F32, **kw)


def _inproj_kernel(x_ref, g_ref, w_ref, wvt_ref, c_ref, s_ref,
                   aq_ref, ak_ref, avt_ref, bq_ref, bk_ref, bvt_ref, cu_ref, cv_ref, *, tm):
    h = _rms(x_ref[...], g_ref[...]).astype(BF16)

    def proj(lo, hi):
        return jnp.dot(h, w_ref[:, lo:hi], preferred_element_type=F32)

    c, s = c_ref[...], s_ref[...]
    first_half = lax.broadcasted_iota(jnp.int32, s.shape, 1) % DIFF_QK_DIM < ROPE_DIM // 2
    s1 = jnp.where(first_half, s, 0.0)
    s2 = s - s1

    def rot(t):
        outs = []
        for hh in range(4):
            th = t[:, hh * 128:(hh + 1) * 128]
            outs.append(th * c + pltpu.roll(th, 120, 1) * s1 + pltpu.roll(th, 8, 1) * s2)
        return jnp.concatenate(outs, axis=1)

    aq_ref[...] = (rot(proj(0, 512)) * (0.125 * LOG2E)).astype(BF16)
    ak_ref[...] = rot(proj(512, 1024)).astype(BF16)
    bq_ref[...] = (proj(1536, 1792) * 0.125).astype(BF16)
    bk_ref[...] = proj(1792, 2048).astype(BF16)
    cu_ref[...] = jax.nn.gelu(proj(2304, 2560))
    cv_ref[...] = jax.nn.gelu(proj(2560, 2816))
    vt = _nt_dot(wvt_ref[...], h).astype(BF16)
    for blk in range(tm // DIFF_TILE):
        avt_ref[blk] = vt[0:WIDTH_A, blk * DIFF_TILE:(blk + 1) * DIFF_TILE]
    for blk in range(tm // SB_TILE):
        bvt_ref[blk] = vt[WIDTH_A:, blk * SB_TILE:(blk + 1) * SB_TILE]


def _inproj(x2, g, w_bf, wvt_bf, rc, rs):
    t = x2.shape[0]
    tm = ROW_TILE
    row = lambda w: pl.BlockSpec((tm, w), lambda i: (i, 0))
    full = lambda a: pl.BlockSpec(a.shape, lambda i: (0,) * a.ndim)
    slab = lambda w, tile: pl.BlockSpec((tm // tile, w, tile), lambda i: (i, 0, 0))
    sds = jax.ShapeDtypeStruct
    outs = [sds((t, 512), BF16), sds((t, 512), BF16), sds((t // DIFF_TILE, WIDTH_A, DIFF_TILE), BF16),
            sds((t, 256), BF16), sds((t, 256), BF16), sds((t // SB_TILE, WIDTH_B, SB_TILE), BF16),
            sds((t, 256), F32), sds((t, 256), F32)]
    return pl.pallas_call(
        functools.partial(_inproj_kernel, tm=tm),
        grid=(t // tm,),
        in_specs=[row(D_MODEL), full(g), full(w_bf), full(wvt_bf), row(128), row(128)],
        out_specs=[row(512), row(512), slab(WIDTH_A, DIFF_TILE), row(256), row(256), slab(WIDTH_B, SB_TILE),
                   row(256), row(256)],
        out_shape=outs,
        compiler_params=pltpu.CompilerParams(dimension_semantics=("parallel",), vmem_limit_bytes=VMEM_LIMIT),
        name="inproj",
    )(x2, g, w_bf, wvt_bf, rc, rs)


def _stack_queries(q_ref, qs_ref, tq):
    q = q_ref[...]
    lane = lax.broadcasted_iota(jnp.int32, (tq, 128), 1)
    zero = jnp.zeros_like(q)
    qs_ref[0:tq, :] = jnp.where(lane < 64, q, zero)
    qs_ref[tq:, :] = jnp.where(lane >= 64, q, zero)


def _fold8(x, op):
    n, w = x.shape
    return op(x.reshape(n // 8, 8, w), axis=0)


def _key_le_query(nk, cw, q0, strict):
    key = lax.broadcasted_iota(jnp.int32, (nk, cw), 0)
    qry = lax.broadcasted_iota(jnp.int32, (nk, cw), 1) + q0
    return key < qry if strict else key <= qry


def _diff_kernel(lam_ref, g_ref, q_ref, k_ref, vt_ref, o_ref, qs_ref, m_ref, l_ref, acc_ref,
                 s_buf, p_buf, a_buf, *, tq, cw, lam_init):
    i = pl.program_id(2)
    chunks = [slice(c * cw, (c + 1) * cw) for c in range(2 * tq // cw)]
    _stack_queries(q_ref, qs_ref, tq)
    m_ref[...] = jnp.full_like(m_ref, -jnp.inf)
    l_ref[...] = jnp.zeros_like(l_ref)
    acc_ref[...] = jnp.zeros_like(acc_ref)
    p_buf[...] = jnp.zeros_like(p_buf)
    a_buf[...] = jnp.ones_like(a_buf)

    def scores(j):
        k = k_ref[pl.ds(pl.multiple_of(j * tq, tq), tq), :]
        return [_nt_dot(k, qs_ref[cs, :]) for cs in chunks]

    def hand_off(s_new):
        for c, cs in enumerate(chunks):
            s_buf[:, cs] = s_new[c]

    def softmax(masked):
        for c, cs in enumerate(chunks):
            q0 = (c * cw) % tq
            nk = q0 + cw if masked else tq
            slabs = [slice(r, r + SOFTMAX_SLAB) for r in range(0, nk, SOFTMAX_SLAB)]

            def load(rs):
                sc = s_buf[rs, cs]
                if masked and rs.stop > q0:
                    keep = _key_le_query(SOFTMAX_SLAB, cw, q0 - rs.start, False)
                    sc = jnp.where(keep, sc, -jnp.inf)
                return sc

            m_old = m_ref[:, cs]
            m8 = functools.reduce(jnp.maximum, [_fold8(load(rs), jnp.max) for rs in slabs])
            m_new = jnp.maximum(m_old, jnp.max(m8, axis=0, keepdims=True))
            alpha = jnp.exp2(m_old - m_new)
            l8 = []
            for rs in slabs:
                p = jnp.exp2(load(rs) - m_new)
                l8.append(_fold8(p, jnp.sum))
                p_buf[rs, cs] = p.astype(BF16)
            l_ref[:, cs] = alpha * l_ref[:, cs] + jnp.sum(functools.reduce(jnp.add, l8), axis=0, keepdims=True)
            m_ref[:, cs] = m_new
            if nk < tq:
                p_buf[nk:, cs] = jnp.zeros((tq - nk, cw), BF16)
            a_buf[:, cs] = alpha

    def values_matmul(j):
        vt = vt_ref[jnp.maximum(j, 0)]
        return [(a_buf[:, cs], jnp.dot(vt, p_buf[:, cs], preferred_element_type=F32)) for cs in chunks]

    def accumulate(scaled):
        for cs, (alpha, pv) in zip(chunks, scaled):
            acc_ref[:, cs] = alpha * acc_ref[:, cs] + pv

    def values(j):
        accumulate(values_matmul(j))

    hand_off(scores(0))

    def body(t, carry):
        s_new = scores(t + 1)
        scaled = values_matmul(t - 1)
        softmax(False)
        accumulate(scaled)
        hand_off(s_new)
        return carry

    lax.fori_loop(0, i, body, 0)
    values(i - 1)
    softmax(True)
    values(i)

    lp = lam_ref[...]
    lam = (jnp.exp(jnp.sum(lp[0:1] * lp[1:2], axis=-1, keepdims=True))
           - jnp.exp(jnp.sum(lp[2:3] * lp[3:4], axis=-1, keepdims=True)) + lam_init)
    o = acc_ref[...] / l_ref[...]
    o = o[:, 0:tq] - lam * o[:, tq:]
    o = o * lax.rsqrt(jnp.mean(o * o, axis=0, keepdims=True) + EPS)
    o_ref[...] = (o.T * g_ref[...] * (1.0 - lam_init)).astype(o_ref.dtype)


def _diff_attention(aq, ak, avt, lam_p, subln_g, lam_init, batch, seq):
    tq = DIFF_TILE
    nq = seq // tq
    kern = functools.partial(_diff_kernel, tq=tq, cw=ATT_CHUNK, lam_init=lam_init)
    qspec = pl.BlockSpec((tq, 128), lambda b, h, i: (b * nq + i, h))
    kspec = pl.BlockSpec((seq, 128), lambda b, h, i: (b, h))
    vspec = pl.BlockSpec((nq, 128, tq), lambda b, h, i: (b, h, 0))
    return pl.pallas_call(
        kern,
        grid=(batch, N_HEADS_A, nq),
        in_specs=[pl.BlockSpec(lam_p.shape, lambda b, h, i: (0, 0)),
                  pl.BlockSpec(subln_g.shape, lambda b, h, i: (0, 0)),
                  qspec, kspec, vspec],
        out_specs=qspec,
        out_shape=jax.ShapeDtypeStruct(aq.shape, BF16),
        scratch_shapes=[pltpu.VMEM((2 * tq, 128), BF16), pltpu.VMEM((1, 2 * tq), F32),
                        pltpu.VMEM((1, 2 * tq), F32), pltpu.VMEM((128, 2 * tq), F32),
                        pltpu.VMEM((tq, 2 * tq), F32), pltpu.VMEM((tq, 2 * tq), BF16),
                        pltpu.VMEM((1, 2 * tq), F32)],
        compiler_params=pltpu.CompilerParams(dimension_semantics=("parallel", "parallel", "arbitrary"),
                                             vmem_limit_bytes=VMEM_LIMIT),
        name="diff_attn",
    )(lam_p, subln_g, aq, ak, avt)


def _sb_kernel(g_ref, u_ref, q_ref, k_ref, vt_ref, o_ref, qs_ref, c_ref, acc_ref,
               z_buf, hi_buf, lo_buf, r_buf, w_buf, *, tq, cw):
    i = pl.program_id(2)
    chunks = [slice(c * cw, (c + 1) * cw) for c in range(2 * tq // cw)]
    _stack_queries(q_ref, qs_ref, tq)
    c_ref[...] = jnp.zeros_like(c_ref)
    acc_ref[...] = jnp.zeros_like(acc_ref)
    w_buf[...] = jnp.zeros_like(w_buf)

    def scores(n):
        k = k_ref[pl.ds(pl.multiple_of((i - n) * tq, tq), tq), :]
        return [_nt_dot(k, qs_ref[cs, :]) for cs in chunks]

    def hand_off(z_new):
        for c, cs in enumerate(chunks):
            z_buf[:, cs] = z_new[c]

    def breaks(masked):
        for c, cs in enumerate(chunks):
            z = z_buf[:, cs]
            sp = jnp.maximum(z, 0.0) + jnp.log(1.0 + jnp.exp2(jnp.abs(z) * -LOG2E))
            prior = c_ref[:, cs]
            rest = z - sp - prior
            if masked:
                strict = _key_le_query(tq, cw, (c * cw) % tq, True)
                sp = jnp.where(strict, sp, 0.0)
                rest = jnp.where(strict, rest, -jnp.inf)
            hi = sp.astype(BF16)
            hi_buf[:, cs] = hi
            lo_buf[:, cs] = (sp - hi.astype(F32)).astype(BF16)
            r_buf[:, cs] = rest
            c_ref[:, cs] = prior + jnp.sum(sp, axis=0, keepdims=True)

    def weights():
        u = u_ref[...]
        for cs in chunks:
            cum = (jnp.dot(u, hi_buf[:, cs], preferred_element_type=F32)
                   + jnp.dot(u, lo_buf[:, cs], preferred_element_type=F32))
            w_buf[:, cs] = jnp.exp(r_buf[:, cs] - cum).astype(BF16)

    def values(n):
        vt = vt_ref[i - jnp.maximum(n, 0)]
        for cs in chunks:
            acc_ref[:, cs] += jnp.dot(vt, w_buf[:, cs], preferred_element_type=F32)

    hand_off(scores(0))
    breaks(True)
    hand_off(scores(jnp.minimum(1, i)))

    def body(n, carry):
        values(n - 2)
        weights()
        breaks(False)
        hand_off(scores(jnp.minimum(n + 1, i)))
        return carry

    lax.fori_loop(1, i + 1, body, 0)
    values(i - 1)
    weights()
    values(i)

    def head_norm(o):
        return o * lax.rsqrt(jnp.mean(o * o, axis=0, keepdims=True) + EPS)

    o = jnp.concatenate([head_norm(acc_ref[0:64, 0:tq]), head_norm(acc_ref[64:128, tq:])], axis=0)
    o_ref[...] = (o.T * g_ref[...]).astype(o_ref.dtype)


def _stick_breaking(bq, bk, bvt, norm_g, batch, seq):
    tq = SB_TILE
    nq = seq // tq
    u = (lax.broadcasted_iota(jnp.int32, (tq, tq), 1) > lax.broadcasted_iota(jnp.int32, (tq, tq), 0)).astype(BF16)
    kern = functools.partial(_sb_kernel, tq=tq, cw=ATT_CHUNK)
    qspec = pl.BlockSpec((tq, 128), lambda b, h, i: (b * nq + i, h))
    kspec = pl.BlockSpec((seq, 128), lambda b, h, i: (b, h))
    vspec = pl.BlockSpec((nq, 128, tq), lambda b, h, i: (b, h, 0))
    return pl.pallas_call(
        kern,
        grid=(batch, N_HEADS_B // 2, nq),
        in_specs=[pl.BlockSpec((1, 128), lambda b, h, i: (0, h)),
                  pl.BlockSpec(u.shape, lambda b, h, i: (0, 0)),
                  qspec, kspec, vspec],
        out_specs=qspec,
        out_shape=jax.ShapeDtypeStruct(bq.shape, BF16),
        scratch_shapes=[pltpu.VMEM((2 * tq, 128), BF16), pltpu.VMEM((1, 2 * tq), F32),
                        pltpu.VMEM((128, 2 * tq), F32),
                        pltpu.VMEM((tq, 2 * tq), F32), pltpu.VMEM((tq, 2 * tq), BF16),
                        pltpu.VMEM((tq, 2 * tq), BF16), pltpu.VMEM((tq, 2 * tq), F32),
                        pltpu.VMEM((tq, 2 * tq), BF16)],
        compiler_params=pltpu.CompilerParams(dimension_semantics=("parallel", "parallel", "arbitrary"),
                                             vmem_limit_bytes=VMEM_LIMIT),
        name="stick_breaking",
    )(norm_g, u, bq, bk, bvt)


def _mixout_kernel(x_ref, ya_ref, yb_ref, cu_ref, cv_ref, sg_ref, ws_ref, bs_ref, nc_ref, wo_ref, o_ref, *, tm):
    vn = _group_rms64(cv_ref[...], sg_ref[...]).astype(BF16)
    tril = lax.broadcasted_iota(jnp.int32, (CHUNK, CHUNK), 0) >= lax.broadcasted_iota(jnp.int32, (CHUNK, CHUNK), 1)
    ws = [jnp.where(tril, ws_ref[g], 0.0).astype(BF16) for g in range(N_GROUPS_C)]
    grp = lax.broadcasted_iota(jnp.int32, (CHUNK, WIDTH_C), 1) // 64
    bias = bs_ref[...]
    chunks = []
    for n in range(tm // CHUNK):
        vc = vn[n * CHUNK:(n + 1) * CHUNK, :]
        s = jnp.zeros((CHUNK, WIDTH_C), F32)
        for g in range(N_GROUPS_C):
            s = jnp.where(grp == g, jnp.dot(ws[g], vc, preferred_element_type=F32), s)
        chunks.append(s + bias)
    yc = cu_ref[...] * jnp.concatenate(chunks, axis=0)
    yc = _group_rms64(yc, nc_ref[...]).astype(BF16)
    o_ref[...] = (x_ref[...]
                  + jnp.dot(ya_ref[...], wo_ref[0:512, :], preferred_element_type=F32)
                  + jnp.dot(yb_ref[...], wo_ref[512:768, :], preferred_element_type=F32)
                  + jnp.dot(yc, wo_ref[768:1024, :], preferred_element_type=F32))


def _mixout(x2, ya, yb, cu, cv, sgu_g, w_s, b_full, nc_g, wo_bf):
    t = x2.shape[0]
    tm = ROW_TILE
    row = lambda w: pl.BlockSpec((tm, w), lambda i: (i, 0))
    full = lambda a: pl.BlockSpec(a.shape, lambda i: (0,) * a.ndim)
    return pl.pallas_call(
        functools.partial(_mixout_kernel, tm=tm),
        grid=(t // tm,),
        in_specs=[row(D_MODEL), row(512), row(256), row(256), row(256),
                  full(sgu_g), full(w_s), full(b_full), full(nc_g), full(wo_bf)],
        out_specs=row(D_MODEL),
        out_shape=jax.ShapeDtypeStruct(x2.shape, F32),
        compiler_params=pltpu.CompilerParams(dimension_semantics=("parallel",), vmem_limit_bytes=VMEM_LIMIT),
        name="sgu_outproj",
    )(x2, ya, yb, cu, cv, sgu_g, w_s, b_full, nc_g, wo_bf)


def _memkv_kernel(m_ref, g_ref, wk_ref, wv_ref, k_ref, v_ref):
    h = _rms(m_ref[0], g_ref[0]).astype(BF16)
    k_ref[0, 0] = jnp.dot(h, wk_ref[0].astype(BF16), preferred_element_type=F32).astype(BF16)
    v_ref[0, 0] = jnp.dot(h, wv_ref[0].astype(BF16), preferred_element_type=F32).astype(BF16)


def _memkv(mem, norm_mem, w_ck, w_cv):
    depth = w_ck.shape[0]
    batch = mem.shape[0]
    out = jax.ShapeDtypeStruct((depth, batch, MEM_LEN, CROSS_WIDTH), BF16)
    wspec = pl.BlockSpec((1, D_MODEL, CROSS_WIDTH), lambda l, b: (l, 0, 0))
    ospec = pl.BlockSpec((1, 1, MEM_LEN, CROSS_WIDTH), lambda l, b: (l, b, 0, 0))
    return pl.pallas_call(
        _memkv_kernel,
        grid=(depth, batch),
        in_specs=[pl.BlockSpec((1, MEM_LEN, D_MODEL), lambda l, b: (b, 0, 0)),
                  pl.BlockSpec((1, 1, D_MODEL), lambda l, b: (l, 0, 0)), wspec, wspec],
        out_specs=[ospec, ospec],
        out_shape=[out, out],
        compiler_params=pltpu.CompilerParams(dimension_semantics=("parallel", "parallel"),
                                             vmem_limit_bytes=VMEM_LIMIT),
        name="mem_kv",
    )(mem, norm_mem.reshape(depth, 1, D_MODEL), w_ck, w_cv)


def _cross_kernel(x_ref, nc_ref, wq_ref, km_ref, vm_ref, wo_ref, nf_ref, wrt_ref, brt_ref, u_ref,
                  o_ref, cls_ref, rank_ref, cnt_ref, carry_ref, *, tm):
    @pl.when(pl.program_id(0) == 0)
    def _():
        carry_ref[...] = jnp.zeros_like(carry_ref)

    x = x_ref[...]
    h = _rms(x, nc_ref[...]).astype(BF16)
    q = (jnp.dot(h, wq_ref[...], preferred_element_type=F32) * 0.125).astype(BF16)
    km = km_ref[0]
    vm = vm_ref[0]
    qgrp = lax.broadcasted_iota(jnp.int32, (tm, CROSS_WIDTH), 1) // 64
    vgrp = lax.broadcasted_iota(jnp.int32, (MEM_LEN, CROSS_WIDTH), 1) // 64
    o = jnp.zeros((tm, CROSS_WIDTH), F32)
    for hh in range(4):
        s = _nt_dot(jnp.where(qgrp == hh, q, jnp.zeros_like(q)), km)
        p = jnp.exp(s - jnp.max(s, axis=-1, keepdims=True))
        p = p / jnp.sum(p, axis=-1, keepdims=True)
        o = o + jnp.dot(p.astype(BF16), jnp.where(vgrp == hh, vm, jnp.zeros_like(vm)),
                        preferred_element_type=F32)
    x2 = x + jnp.dot(o.astype(BF16), wo_ref[...], preferred_element_type=F32)
    o_ref[...] = x2

    hf = _rms(x2, nf_ref[...])
    lt = _nt_dot(wrt_ref[...], hf, precision=lax.Precision.HIGHEST) + brt_ref[...]
    gl = [lt[r:r + 1, :] for r in range(4)]
    el = [lt[4 + r:5 + r, :] for r in range(16)]

    def first_argmax(vals):
        mx = functools.reduce(jnp.maximum, vals)
        idx = jnp.full(vals[0].shape, len(vals) - 1, jnp.int32)
        for r in range(len(vals) - 2, -1, -1):
            idx = jnp.where(vals[r] == mx, r, idx)
        return idx

    g = first_argmax(gl)
    es = [jnp.where(g == 0, el[r], jnp.where(g == 1, el[4 + r], jnp.where(g == 2, el[8 + r], el[12 + r])))
          for r in range(4)]
    i1 = first_argmax(es)
    i2 = first_argmax([jnp.where(i1 == r, -jnp.inf, es[r]) for r in range(4)])
    a = jnp.minimum(i1, i2)
    b = jnp.maximum(i1, i2)
    cls = g * 6 + jnp.where(a == 0, b - 1, jnp.where(a == 1, b + 1, 5))

    onehot = lax.broadcasted_iota(jnp.int32, (CLASS_ROWS, tm), 0) == cls
    oh = jnp.where(onehot, 1.0, 0.0)
    before = jnp.dot(oh.astype(BF16), u_ref[...], preferred_element_type=F32) + carry_ref[...]
    rank = jnp.sum(jnp.where(onehot, before, 0.0), axis=0, keepdims=True)
    carry_ref[...] += jnp.sum(oh, axis=1, keepdims=True)
    cls_ref[0] = cls
    rank_ref[0] = rank.astype(jnp.int32)
    cnt_ref[...] = jnp.broadcast_to(carry_ref[...], cnt_ref.shape).astype(jnp.int32)


def _cross_router(x2, nc_g, wq_bf, kmem, vmem, wo_bf, nf_g, wrt_t, brt_col, seq):
    t = x2.shape[0]
    tm = ROW_TILE
    nt = t // tm
    per_b = seq // tm
    u = (lax.broadcasted_iota(jnp.int32, (tm, tm), 0) < lax.broadcasted_iota(jnp.int32, (tm, tm), 1)).astype(BF16)
    row = pl.BlockSpec((tm, D_MODEL), lambda i: (i, 0))
    full = lambda a: pl.BlockSpec(a.shape, lambda i: (0,) * a.ndim)
    mspec = pl.BlockSpec((1, MEM_LEN, CROSS_WIDTH), lambda i: (i // per_b, 0, 0))
    tokspec = pl.BlockSpec((1, 1, tm), lambda i: (i, 0, 0))
    return pl.pallas_call(
        functools.partial(_cross_kernel, tm=tm),
        grid=(nt,),
        in_specs=[row, full(nc_g), full(wq_bf), mspec, mspec, full(wo_bf), full(nf_g), full(wrt_t),
                  full(brt_col), full(u)],
        out_specs=[row, tokspec, tokspec, pl.BlockSpec((CLASS_ROWS, 128), lambda i: (0, 0))],
        out_shape=[jax.ShapeDtypeStruct(x2.shape, F32), jax.ShapeDtypeStruct((nt, 1, tm), jnp.int32),
                   jax.ShapeDtypeStruct((nt, 1, tm), jnp.int32),
                   jax.ShapeDtypeStruct((CLASS_ROWS, 128), jnp.int32)],
        scratch_shapes=[pltpu.VMEM((CLASS_ROWS, 1), F32)],
        compiler_params=pltpu.CompilerParams(dimension_semantics=("arbitrary",), vmem_limit_bytes=VMEM_LIMIT),
        name="cross_router",
    )(x2, nc_g, wq_bf, kmem, vmem, wo_bf, nf_g, wrt_t, brt_col, u)


def _moe_kernel(e1_ref, e2_ref, nv_ref, src_ref,
                x_hbm, nf_ref, wrt_ref, brt_ref, wg1_ref, wu1_ref, wd1_ref, wg2_ref, wu2_ref, wd2_ref,
                nfin_ref, out_hbm, xbuf, obuf, wgc, wuc, wdc, gsem, ssem, *, tile, n_tiles, final):
    k = pl.program_id(0)
    slot = k % 2
    prev = jnp.maximum(k - 1, 0)

    def gather_row(kk, sl, r):
        tok = src_ref[kk * tile + r]
        return pltpu.make_async_copy(x_hbm.at[pl.ds(tok, 1), :], xbuf.at[sl, pl.ds(r, 1), :], gsem.at[sl])

    def scatter_row(kk, sl, r):
        tok = src_ref[kk * tile + r]
        return pltpu.make_async_copy(obuf.at[sl, pl.ds(r, 1), :], out_hbm.at[pl.ds(tok, 1), :], ssem.at[sl])

    def start_gather(kk, sl):
        def body(r, c):
            gather_row(kk, sl, r).start()
            return c
        lax.fori_loop(0, tile, body, 0, unroll=8)

    def wait_gather(sl):
        pltpu.make_async_copy(x_hbm.at[pl.ds(0, tile), :], xbuf.at[sl], gsem.at[sl]).wait()

    def wait_scatter(n, sl):
        n8 = pl.multiple_of((n // 8) * 8, 8)

        @pl.when(n8 > 0)
        def _():
            pltpu.make_async_copy(obuf.at[sl, pl.ds(0, n8), :], out_hbm.at[pl.ds(0, n8), :], ssem.at[sl]).wait()

        def body(r, c):
            pltpu.make_async_copy(obuf.at[sl, pl.ds(0, 1), :], out_hbm.at[pl.ds(0, 1), :], ssem.at[sl]).wait()
            return c
        lax.fori_loop(n8, n, body, 0)

    @pl.when(jnp.logical_and(k == 0, nv_ref[0] > 0))
    def _():
        start_gather(0, 0)

    @pl.when(k >= 2)
    def _():
        wait_scatter(nv_ref[jnp.maximum(k - 2, 0)], slot)

    nxt = jnp.minimum(k + 1, n_tiles - 1)

    @pl.when(jnp.logical_and(k + 1 < n_tiles, nv_ref[nxt] > 0))
    def _():
        start_gather(k + 1, 1 - slot)

    @pl.when(nv_ref[k] > 0)
    def _():
        @pl.when(jnp.logical_or(k == 0, e1_ref[k] != e1_ref[prev]))
        def _():
            wgc[0] = wg1_ref[0, 0].astype(BF16)
            wuc[0] = wu1_ref[0, 0].astype(BF16)
            wdc[0] = wd1_ref[0, 0].astype(BF16)

        @pl.when(jnp.logical_or(k == 0, e2_ref[k] != e2_ref[prev]))
        def _():
            wgc[1] = wg2_ref[0, 0].astype(BF16)
            wuc[1] = wu2_ref[0, 0].astype(BF16)
            wdc[1] = wd2_ref[0, 0].astype(BF16)

        wait_gather(slot)
        x = xbuf[slot]
        hf = _rms(x, nf_ref[...])
        h = hf.astype(BF16)
        lg = jnp.dot(hf, wrt_ref[...], preferred_element_type=F32, precision=lax.Precision.HIGHEST) + brt_ref[...]
        lane = lax.broadcasted_iota(jnp.int32, lg.shape, 1)
        e1 = e1_ref[k]
        e2 = e2_ref[k]
        glm = jnp.where(lane < N_EXPERT_GROUPS, lg, -jnp.inf)
        eg = jnp.exp(glm - jnp.max(glm, axis=-1, keepdims=True))
        pick = lambda a, idx: jnp.sum(jnp.where(lane == idx, a, 0.0), axis=-1, keepdims=True)
        p_group = pick(eg, e1 // EXPERTS_PER_GROUP) / jnp.sum(eg, axis=-1, keepdims=True)
        la = pick(lg, N_EXPERT_GROUPS + e1)
        lb = pick(lg, N_EXPERT_GROUPS + e2)
        mx = jnp.maximum(la, lb)
        ea = jnp.exp(la - mx)
        eb = jnp.exp(lb - mx)
        wa = ea / (ea + eb) * p_group
        wb = eb / (ea + eb) * p_group

        def expert(s):
            gt = jnp.dot(h, wgc[s], preferred_element_type=F32)
            up = jnp.dot(h, wuc[s], preferred_element_type=F32)
            act = (jax.nn.silu(gt) * up).astype(BF16)
            return jnp.dot(act, wdc[s], preferred_element_type=F32)

        y = x + (wa * expert(0) + wb * expert(1))
        if final:
            y = _rms(y, nfin_ref[...])
        obuf[slot] = y

        n_rows = nv_ref[k]

        def body8(g, c):
            for r in range(8):
                scatter_row(k, slot, g * 8 + r).start()
            return c
        lax.fori_loop(0, n_rows // 8, body8, 0)

        def body(r, c):
            scatter_row(k, slot, r).start()
            return c
        lax.fori_loop((n_rows // 8) * 8, n_rows, body, 0)

    @pl.when(k == n_tiles - 1)
    def _():
        wait_scatter(nv_ref[k], slot)
        wait_scatter(nv_ref[prev], 1 - slot)


def _moe(x2, e1, e2, nv, src, nf_g, wrt, brt_row, w_gate, w_up, w_down, layer, nfin_g, final):
    t = x2.shape[0]
    tile = MOE_TILE
    n_tiles = e1.shape[0]
    full = lambda a: pl.BlockSpec(a.shape, lambda k, *_: (0,) * a.ndim)
    wspec1 = lambda a: pl.BlockSpec((1, 1) + a.shape[2:], lambda k, e1r, e2r, nvr, srcr: (layer, e1r[k], 0, 0))
    wspec2 = lambda a: pl.BlockSpec((1, 1) + a.shape[2:], lambda k, e1r, e2r, nvr, srcr: (layer, e2r[k], 0, 0))
    anyspec = pl.BlockSpec(memory_space=pl.ANY)
    grid_spec = pltpu.PrefetchScalarGridSpec(
        num_scalar_prefetch=4,
        grid=(n_tiles,),
        in_specs=[anyspec, full(nf_g), full(wrt), full(brt_row),
                  wspec1(w_gate), wspec1(w_up), wspec1(w_down),
                  wspec2(w_gate), wspec2(w_up), wspec2(w_down), full(nfin_g)],
        out_specs=anyspec,
        scratch_shapes=[pltpu.VMEM((2, tile, D_MODEL), F32), pltpu.VMEM((2, tile, D_MODEL), F32),
                        pltpu.VMEM((2, D_MODEL, D_EXPERT), BF16), pltpu.VMEM((2, D_MODEL, D_EXPERT), BF16),
                        pltpu.VMEM((2, D_EXPERT, D_MODEL), BF16),
                        pltpu.SemaphoreType.DMA((2,)), pltpu.SemaphoreType.DMA((2,))],
    )
    return pl.pallas_call(
        functools.partial(_moe_kernel, tile=tile, n_tiles=n_tiles, final=final),
        grid_spec=grid_spec,
        out_shape=jax.ShapeDtypeStruct((t, D_MODEL), F32),
        compiler_params=pltpu.CompilerParams(dimension_semantics=("arbitrary",), vmem_limit_bytes=VMEM_LIMIT),
        name="moe_experts",
    )(e1, e2, nv, src, x2, nf_g, wrt, brt_row, w_gate, w_up, w_down, w_gate, w_up, w_down, nfin_g)


def _route_meta(cls, rank, counts, t, tile, n_tiles):
    tiles_per = (counts + tile - 1) // tile
    tile_end = jnp.cumsum(tiles_per)
    tile_start = tile_end - tiles_per
    pos = tile_start[cls] * tile + rank
    src = jnp.zeros((n_tiles * tile,), jnp.int32).at[pos].set(jnp.arange(t, dtype=jnp.int32), unique_indices=True)
    kk = jnp.arange(n_tiles, dtype=jnp.int32)
    tcls = jnp.sum((kk[:, None] >= tile_end[None, :]).astype(jnp.int32), axis=1)
    valid = tcls < N_CLASSES
    last_cls = tcls[jnp.maximum(tile_end[-1] - 1, 0)]
    tc = jnp.minimum(jnp.where(valid, tcls, last_cls), N_CLASSES - 1)
    nv = jnp.where(valid, jnp.clip(counts[tc] - (kk - tile_start[tc]) * tile, 0, tile), 0)
    grp = tc // 6
    e1 = grp * EXPERTS_PER_GROUP + jnp.asarray(PAIR_LO, jnp.int32)[tc % 6]
    e2 = grp * EXPERTS_PER_GROUP + jnp.asarray(PAIR_HI, jnp.int32)[tc % 6]
    return e1.astype(jnp.int32), e2.astype(jnp.int32), nv.astype(jnp.int32), src


def _rotary_lane_tables(positions):
    lane = jnp.arange(128, dtype=jnp.int32) % DIFF_QK_DIM
    half = ROPE_DIM // 2
    inv_freq = ROPE_THETA ** (-(2 * (lane % half)).astype(F32) / ROPE_DIM)
    inv_freq = jnp.where(lane < ROPE_DIM, inv_freq, 0.0)
    sign = jnp.where(lane < half, -1.0, 1.0).astype(F32)
    ang = positions.astype(F32).reshape(-1, 1) * inv_freq[None, :]
    return jnp.cos(ang), jnp.sin(ang) * sign[None, :]


def kernel(x, mem, positions, norm_mix, w_in, lam_q1, lam_k1, lam_q2, lam_k2, subln_a, norm_b_out, sgu_norm, w_spatial, b_spatial, norm_c_out, w_out, norm_cross, norm_mem, w_cq, w_ck, w_cv, w_co, norm_ffn, w_group, b_group, w_router, b_router, w_gate, w_up, w_down, norm_final):
    batch, seq, d = x.shape
    depth = w_in.shape[0]
    t = batch * seq
    n_tiles = t // MOE_TILE + N_CLASSES
    x2 = x.reshape(t, d)
    rc, rs = _rotary_lane_tables(positions)
    kmem, vmem = _memkv(mem, norm_mem, w_ck, w_cv)
    row = lambda a: a.reshape(1, -1)
    for l in range(depth):
        lam_init = 0.8 - 0.6 * math.exp(-0.3 * l)
        w_vt = jnp.concatenate([w_in[l][:, 1024:1536], w_in[l][:, 2048:2304]], axis=1).T.astype(BF16)
        aq, ak, avt, bq, bk, bvt, cu, cv = _inproj(x2, row(norm_mix[l]), w_in[l].astype(BF16), w_vt, rc, rs)
        lam_p = jnp.stack([lam_q1[l], lam_k1[l], lam_q2[l], lam_k2[l]])
        ya = _diff_attention(aq, ak, avt, lam_p, row(subln_a[l]), lam_init, batch, seq)
        yb = _stick_breaking(bq, bk, bvt, row(norm_b_out[l]), batch, seq)
        b_full = jnp.repeat(b_spatial[l].T, 64, axis=1)
        x2 = _mixout(x2, ya, yb, cu, cv, row(sgu_norm[l]), w_spatial[l], b_full, row(norm_c_out[l]),
                     w_out[l].astype(BF16))
        wrt = jnp.concatenate([w_group[l], w_router[l],
                               jnp.zeros((d, ROUTER_LANES - N_EXPERT_GROUPS - N_EXPERTS), F32)], axis=1)
        brt = jnp.concatenate([b_group[l], b_router[l],
                               jnp.zeros((ROUTER_LANES - N_EXPERT_GROUPS - N_EXPERTS,), F32)])
        x2, cls, rank, cnt = _cross_router(x2, row(norm_cross[l]), w_cq[l].astype(BF16), kmem[l], vmem[l],
                                           w_co[l].astype(BF16), row(norm_ffn[l]), wrt.T, brt.reshape(-1, 1), seq)
        e1, e2, nv, src = _route_meta(cls.reshape(t), rank.reshape(t), cnt[:N_CLASSES, 0], t, MOE_TILE, n_tiles)
        x2 = _moe(x2, e1, e2, nv, src, row(norm_ffn[l]), wrt, brt.reshape(1, -1), w_gate, w_up, w_down, l,
                  row(norm_final), final=(l == depth - 1))
    return x2.reshape(batch, seq, d)
```

```python
import functools
import math

import jax
import jax.numpy as jnp
from jax import lax
from jax.experimental import pallas as pl
from jax.experimental.pallas import tpu as pltpu

F32 = jnp.float32
BF16 = jnp.bfloat16
EPS = 1e-6

D_MODEL = 1024
N_HEADS_A = 4
DIFF_QK_DIM = 64
N_HEADS_B = 4
SB_DIM = 64
N_GROUPS_C = 4
CHUNK = 128
WIDTH_A = 512
WIDTH_B = 256
WIDTH_C = 256
D_IN = 2816
ROPE_THETA = 500000.0
ROPE_DIM = 16
MEM_LEN = 256
CROSS_WIDTH = 256
N_EXPERT_GROUPS = 4
EXPERTS_PER_GROUP = 4
N_EXPERTS = 16
D_EXPERT = 512
N_CLASSES = 24
CLASS_ROWS = 32
ROUTER_LANES = 128
PAIR_LO = (0, 0, 0, 1, 1, 2)
PAIR_HI = (1, 2, 3, 2, 3, 3)

ROW_TILE = 512
DIFF_TILE = 512
SB_TILE = 256
ATT_CHUNK = 256
SOFTMAX_SLAB = 128
LOG2E = 1.4426950408889634
SB_DEAD_LOG = 110.0
MOE_TILE = 256
VMEM_LIMIT = 56 * 1024 * 1024


def _rms(x, g):
    return x * lax.rsqrt(jnp.mean(x * x, axis=-1, keepdims=True) + EPS) * g


def _group_rms64(x, g):
    m, w = x.shape
    grp = lax.broadcasted_iota(jnp.int32, (m, w), 1) // 64
    sq = x * x
    scale = jnp.zeros_like(x)
    for gi in range(w // 64):
        sel = grp == gi
        ss = jnp.sum(jnp.where(sel, sq, 0.0), axis=-1, keepdims=True)
        scale = jnp.where(sel, lax.rsqrt(ss * (1.0 / 64.0) + EPS), scale)
    return x * scale * g


def _nt_dot(a, b, **kw):
    return lax.dot_general(a, b, (((1,), (1,)), ((), ())), preferred_element_type=F32, **kw)


def _inproj_kernel(x_ref, g_ref, w_ref, wvt_ref, c_ref, s_ref,
                   aq_ref, ak_ref, avt_ref, bq_ref, bk_ref, bvt_ref, cu_ref, cv_ref, *, tm):
    h = _rms(x_ref[...], g_ref[...]).astype(BF16)

    def proj(lo, hi):
        return jnp.dot(h, w_ref[:, lo:hi], preferred_element_type=F32)

    c, s = c_ref[...], s_ref[...]
    first_half = lax.broadcasted_iota(jnp.int32, s.shape, 1) % DIFF_QK_DIM < ROPE_DIM // 2
    s1 = jnp.where(first_half, s, 0.0)
    s2 = s - s1

    def rot(t):
        outs = []
        for hh in range(4):
            th = t[:, hh * 128:(hh + 1) * 128]
            outs.append(th * c + pltpu.roll(th, 120, 1) * s1 + pltpu.roll(th, 8, 1) * s2)
        return jnp.concatenate(outs, axis=1)

    aq_ref[...] = (rot(proj(0, 512)) * (0.125 * LOG2E)).astype(BF16)
    ak_ref[...] = rot(proj(512, 1024)).astype(BF16)
    bq_ref[...] = (proj(1536, 1792) * 0.125).astype(BF16)
    bk_ref[...] = proj(1792, 2048).astype(BF16)
    cu_ref[...] = jax.nn.gelu(proj(2304, 2560))
    cv_ref[...] = jax.nn.gelu(proj(2560, 2816))
    vt = _nt_dot(wvt_ref[...], h).astype(BF16)
    for blk in range(tm // DIFF_TILE):
        avt_ref[blk] = vt[0:WIDTH_A, blk * DIFF_TILE:(blk + 1) * DIFF_TILE]
    for blk in range(tm // SB_TILE):
        bvt_ref[blk] = vt[WIDTH_A:, blk * SB_TILE:(blk + 1) * SB_TILE]


def _inproj(x2, g, w_bf, wvt_bf, rc, rs):
    t = x2.shape[0]
    tm = ROW_TILE
    row = lambda w: pl.BlockSpec((tm, w), lambda i: (i, 0))
    full = lambda a: pl.BlockSpec(a.shape, lambda i: (0,) * a.ndim)
    slab = lambda w, tile: pl.BlockSpec((tm // tile, w, tile), lambda i: (i, 0, 0))
    sds = jax.ShapeDtypeStruct
    outs = [sds((t, 512), BF16), sds((t, 512), BF16), sds((t // DIFF_TILE, WIDTH_A, DIFF_TILE), BF16),
            sds((t, 256), BF16), sds((t, 256), BF16), sds((t // SB_TILE, WIDTH_B, SB_TILE), BF16),
            sds((t, 256), F32), sds((t, 256), F32)]
    return pl.pallas_call(
        functools.partial(_inproj_kernel, tm=tm),
        grid=(t // tm,),
        in_specs=[row(D_MODEL), full(g), full(w_bf), full(wvt_bf), row(128), row(128)],
        out_specs=[row(512), row(512), slab(WIDTH_A, DIFF_TILE), row(256), row(256), slab(WIDTH_B, SB_TILE),
                   row(256), row(256)],
        out_shape=outs,
        compiler_params=pltpu.CompilerParams(dimension_semantics=("parallel",), vmem_limit_bytes=VMEM_LIMIT),
        name="inproj",
    )(x2, g, w_bf, wvt_bf, rc, rs)


def _stack_queries(q_ref, qs_ref, tq):
    q = q_ref[...]
    lane = lax.broadcasted_iota(jnp.int32, (tq, 128), 1)
    zero = jnp.zeros_like(q)
    qs_ref[0:tq, :] = jnp.where(lane < 64, q, zero)
    qs_ref[tq:, :] = jnp.where(lane >= 64, q, zero)


def _fold8(x, op):
    n, w = x.shape
    return op(x.reshape(n // 8, 8, w), axis=0)


def _key_le_query(nk, cw, q0, strict):
    key = lax.broadcasted_iota(jnp.int32, (nk, cw), 0)
    qry = lax.broadcasted_iota(jnp.int32, (nk, cw), 1) + q0
    return key < qry if strict else key <= qry


def _diff_kernel(lam_ref, g_ref, q_ref, k_ref, vt_ref, o_ref, qs_ref, m_ref, l_ref, acc_ref,
                 s_buf, p_buf, a_buf, *, tq, cw, lam_init):
    i = pl.program_id(2)
    chunks = [slice(c * cw, (c + 1) * cw) for c in range(2 * tq // cw)]
    _stack_queries(q_ref, qs_ref, tq)
    m_ref[...] = jnp.full_like(m_ref, -jnp.inf)
    l_ref[...] = jnp.zeros_like(l_ref)
    acc_ref[...] = jnp.zeros_like(acc_ref)
    p_buf[...] = jnp.zeros_like(p_buf)
    a_buf[...] = jnp.ones_like(a_buf)

    def scores(j):
        k = k_ref[pl.ds(pl.multiple_of(j * tq, tq), tq), :]
        return [_nt_dot(k, qs_ref[cs, :]) for cs in chunks]

    def hand_off(s_new):
        for c, cs in enumerate(chunks):
            s_buf[:, cs] = s_new[c]

    def softmax(masked):
        for c, cs in enumerate(chunks):
            q0 = (c * cw) % tq
            nk = q0 + cw if masked else tq
            slabs = [slice(r, r + SOFTMAX_SLAB) for r in range(0, nk, SOFTMAX_SLAB)]

            def load(rs):
                sc = s_buf[rs, cs]
                if masked and rs.stop > q0:
                    keep = _key_le_query(SOFTMAX_SLAB, cw, q0 - rs.start, False)
                    sc = jnp.where(keep, sc, -jnp.inf)
                return sc

            m_old = m_ref[:, cs]
            m8 = functools.reduce(jnp.maximum, [_fold8(load(rs), jnp.max) for rs in slabs])
            m_new = jnp.maximum(m_old, jnp.max(m8, axis=0, keepdims=True))
            alpha = jnp.exp2(m_old - m_new)
            l8 = []
            for rs in slabs:
                p = jnp.exp2(load(rs) - m_new)
                l8.append(_fold8(p, jnp.sum))
                p_buf[rs, cs] = p.astype(BF16)
            l_ref[:, cs] = alpha * l_ref[:, cs] + jnp.sum(functools.reduce(jnp.add, l8), axis=0, keepdims=True)
            m_ref[:, cs] = m_new
            if nk < tq:
                p_buf[nk:, cs] = jnp.zeros((tq - nk, cw), BF16)
            a_buf[:, cs] = alpha

    def values_matmul(j):
        vt = vt_ref[jnp.maximum(j, 0)]
        return [(a_buf[:, cs], jnp.dot(vt, p_buf[:, cs], preferred_element_type=F32)) for cs in chunks]

    def accumulate(scaled):
        for cs, (alpha, pv) in zip(chunks, scaled):
            acc_ref[:, cs] = alpha * acc_ref[:, cs] + pv

    def values(j):
        accumulate(values_matmul(j))

    hand_off(scores(0))

    def body(t, carry):
        s_new = scores(t + 1)
        scaled = values_matmul(t - 1)
        softmax(False)
        accumulate(scaled)
        hand_off(s_new)
        return carry

    lax.fori_loop(0, i, body, 0)
    values(i - 1)
    softmax(True)
    values(i)

    lp = lam_ref[...]
    lam = (jnp.exp(jnp.sum(lp[0:1] * lp[1:2], axis=-1, keepdims=True))
           - jnp.exp(jnp.sum(lp[2:3] * lp[3:4], axis=-1, keepdims=True)) + lam_init)
    o = acc_ref[...] / l_ref[...]
    o = o[:, 0:tq] - lam * o[:, tq:]
    o = o * lax.rsqrt(jnp.mean(o * o, axis=0, keepdims=True) + EPS)
    o_ref[...] = (o.T * g_ref[...] * (1.0 - lam_init)).astype(o_ref.dtype)


def _diff_attention(aq, ak, avt, lam_p, subln_g, lam_init, batch, seq):
    tq = DIFF_TILE
    nq = seq // tq
    kern = functools.partial(_diff_kernel, tq=tq, cw=ATT_CHUNK, lam_init=lam_init)
    qspec = pl.BlockSpec((tq, 128), lambda b, h, i: (b * nq + i, h))
    kspec = pl.BlockSpec((seq, 128), lambda b, h, i: (b, h))
    vspec = pl.BlockSpec((nq, 128, tq), lambda b, h, i: (b, h, 0))
    return pl.pallas_call(
        kern,
        grid=(batch, N_HEADS_A, nq),
        in_specs=[pl.BlockSpec(lam_p.shape, lambda b, h, i: (0, 0)),
                  pl.BlockSpec(subln_g.shape, lambda b, h, i: (0, 0)),
                  qspec, kspec, vspec],
        out_specs=qspec,
        out_shape=jax.ShapeDtypeStruct(aq.shape, BF16),
        scratch_shapes=[pltpu.VMEM((2 * tq, 128), BF16), pltpu.VMEM((1, 2 * tq), F32),
                        pltpu.VMEM((1, 2 * tq), F32), pltpu.VMEM((128, 2 * tq), F32),
                        pltpu.VMEM((tq, 2 * tq), F32), pltpu.VMEM((tq, 2 * tq), BF16),
                        pltpu.VMEM((1, 2 * tq), F32)],
        compiler_params=pltpu.CompilerParams(dimension_semantics=("parallel", "parallel", "arbitrary"),
                                             vmem_limit_bytes=VMEM_LIMIT),
        name="diff_attn",
    )(lam_p, subln_g, aq, ak, avt)


def _sb_kernel(g_ref, u_ref, q_ref, k_ref, vt_ref, o_ref, qs_ref, c_ref, acc_ref,
               z_buf, hi_buf, lo_buf, r_buf, w_buf, lb_ref, flag_ref, *, tq, cw):
    i = pl.program_id(2)
    chunks = [slice(c * cw, (c + 1) * cw) for c in range(2 * tq // cw)]
    _stack_queries(q_ref, qs_ref, tq)
    c_ref[...] = jnp.zeros_like(c_ref)
    acc_ref[...] = jnp.zeros_like(acc_ref)
    w_buf[...] = jnp.zeros_like(w_buf)
    lb_ref[...] = jnp.zeros_like(lb_ref)

    def scores(n):
        k = k_ref[pl.ds(pl.multiple_of((i - n) * tq, tq), tq), :]
        return [_nt_dot(k, qs_ref[cs, :]) for cs in chunks]

    def hand_off(z_new):
        for c, cs in enumerate(chunks):
            z_buf[:, cs] = z_new[c]

    def breaks(masked):
        for c, cs in enumerate(chunks):
            z = z_buf[:, cs]
            sp = jnp.maximum(z, 0.0) + jnp.log(1.0 + jnp.exp2(jnp.abs(z) * -LOG2E))
            prior = c_ref[:, cs]
            rest = z - sp - prior
            if masked:
                strict = _key_le_query(tq, cw, (c * cw) % tq, True)
                sp = jnp.where(strict, sp, 0.0)
                rest = jnp.where(strict, rest, -jnp.inf)
            hi = sp.astype(BF16)
            hi_buf[:, cs] = hi
            lo_buf[:, cs] = (sp - hi.astype(F32)).astype(BF16)
            r_buf[:, cs] = rest
            c_ref[:, cs] = prior + jnp.sum(sp, axis=0, keepdims=True)

    def weights():
        u = u_ref[...]
        for cs in chunks:
            cum = (jnp.dot(u, hi_buf[:, cs], preferred_element_type=F32)
                   + jnp.dot(u, lo_buf[:, cs], preferred_element_type=F32))
            w_buf[:, cs] = jnp.exp(r_buf[:, cs] - cum).astype(BF16)
            lb_ref[:, cs] = lb_ref[:, cs] + cum[0:1]

    def values(n):
        vt = vt_ref[i - jnp.maximum(n, 0)]
        for cs in chunks:
            acc_ref[:, cs] += jnp.dot(vt, w_buf[:, cs], preferred_element_type=F32)

    hand_off(scores(0))
    breaks(True)
    hand_off(scores(jnp.minimum(1, i)))

    flag_ref[0] = 1
    flag_ref[1] = 1

    def body(n, carry):
        @pl.when(flag_ref[0] == 1)
        def _():
            values(n - 2)
            weights()
            breaks(False)
            hand_off(scores(jnp.minimum(n + 1, i)))
            flag_ref[0] = jnp.max(jnp.where(lb_ref[...] <= SB_DEAD_LOG, 1, 0))
            flag_ref[1] = n + 1
        return carry

    lax.fori_loop(1, i + 1, body, 0)
    n_done = flag_ref[1]
    values(n_done - 2)
    weights()
    values(n_done - 1)

    def head_norm(o):
        return o * lax.rsqrt(jnp.mean(o * o, axis=0, keepdims=True) + EPS)

    o = jnp.concatenate([head_norm(acc_ref[0:64, 0:tq]), head_norm(acc_ref[64:128, tq:])], axis=0)
    o_ref[...] = (o.T * g_ref[...]).astype(o_ref.dtype)


def _stick_breaking(bq, bk, bvt, norm_g, batch, seq):
    tq = SB_TILE
    nq = seq // tq
    u = (lax.broadcasted_iota(jnp.int32, (tq, tq), 1) > lax.broadcasted_iota(jnp.int32, (tq, tq), 0)).astype(BF16)
    kern = functools.partial(_sb_kernel, tq=tq, cw=ATT_CHUNK)
    qspec = pl.BlockSpec((tq, 128), lambda b, h, i: (b * nq + i, h))
    kspec = pl.BlockSpec((seq, 128), lambda b, h, i: (b, h))
    vspec = pl.BlockSpec((nq, 128, tq), lambda b, h, i: (b, h, 0))
    return pl.pallas_call(
        kern,
        grid=(batch, N_HEADS_B // 2, nq),
        in_specs=[pl.BlockSpec((1, 128), lambda b, h, i: (0, h)),
                  pl.BlockSpec(u.shape, lambda b, h, i: (0, 0)),
                  qspec, kspec, vspec],
        out_specs=qspec,
        out_shape=jax.ShapeDtypeStruct(bq.shape, BF16),
        scratch_shapes=[pltpu.VMEM((2 * tq, 128), BF16), pltpu.VMEM((1, 2 * tq), F32),
                        pltpu.VMEM((128, 2 * tq), F32),
                        pltpu.VMEM((tq, 2 * tq), F32), pltpu.VMEM((tq, 2 * tq), BF16),
                        pltpu.VMEM((tq, 2 * tq), BF16), pltpu.VMEM((tq, 2 * tq), F32),
                        pltpu.VMEM((tq, 2 * tq), BF16), pltpu.VMEM((1, 2 * tq), F32),
                        pltpu.SMEM((2,), jnp.int32)],
        compiler_params=pltpu.CompilerParams(dimension_semantics=("parallel", "parallel", "arbitrary"),
                                             vmem_limit_bytes=VMEM_LIMIT),
        name="stick_breaking",
    )(norm_g, u, bq, bk, bvt)


def _mixout_kernel(x_ref, ya_ref, yb_ref, cu_ref, cv_ref, sg_ref, ws_ref, bs_ref, nc_ref, wo_ref, o_ref, *, tm):
    vn = _group_rms64(cv_ref[...], sg_ref[...]).astype(BF16)
    tril = lax.broadcasted_iota(jnp.int32, (CHUNK, CHUNK), 0) >= lax.broadcasted_iota(jnp.int32, (CHUNK, CHUNK), 1)
    ws = [jnp.where(tril, ws_ref[g], 0.0).astype(BF16) for g in range(N_GROUPS_C)]
    grp = lax.broadcasted_iota(jnp.int32, (CHUNK, WIDTH_C), 1) // 64
    bias = bs_ref[...]
    chunks = []
    for n in range(tm // CHUNK):
        vc = vn[n * CHUNK:(n + 1) * CHUNK, :]
        s = jnp.zeros((CHUNK, WIDTH_C), F32)
        for g in range(N_GROUPS_C):
            s = jnp.where(grp == g, jnp.dot(ws[g], vc, preferred_element_type=F32), s)
        chunks.append(s + bias)
    yc = cu_ref[...] * jnp.concatenate(chunks, axis=0)
    yc = _group_rms64(yc, nc_ref[...]).astype(BF16)
    o_ref[...] = (x_ref[...]
                  + jnp.dot(ya_ref[...], wo_ref[0:512, :], preferred_element_type=F32)
                  + jnp.dot(yb_ref[...], wo_ref[512:768, :], preferred_element_type=F32)
                  + jnp.dot(yc, wo_ref[768:1024, :], preferred_element_type=F32))


def _mixout(x2, ya, yb, cu, cv, sgu_g, w_s, b_full, nc_g, wo_bf):
    t = x2.shape[0]
    tm = ROW_TILE
    row = lambda w: pl.BlockSpec((tm, w), lambda i: (i, 0))
    full = lambda a: pl.BlockSpec(a.shape, lambda i: (0,) * a.ndim)
    return pl.pallas_call(
        functools.partial(_mixout_kernel, tm=tm),
        grid=(t // tm,),
        in_specs=[row(D_MODEL), row(512), row(256), row(256), row(256),
                  full(sgu_g), full(w_s), full(b_full), full(nc_g), full(wo_bf)],
        out_specs=row(D_MODEL),
        out_shape=jax.ShapeDtypeStruct(x2.shape, F32),
        compiler_params=pltpu.CompilerParams(dimension_semantics=("parallel",), vmem_limit_bytes=VMEM_LIMIT),
        name="sgu_outproj",
    )(x2, ya, yb, cu, cv, sgu_g, w_s, b_full, nc_g, wo_bf)


def _memkv_kernel(m_ref, g_ref, wk_ref, wv_ref, k_ref, v_ref):
    h = _rms(m_ref[0], g_ref[0]).astype(BF16)
    k_ref[0, 0] = jnp.dot(h, wk_ref[0].astype(BF16), preferred_element_type=F32).astype(BF16)
    v_ref[0, 0] = jnp.dot(h, wv_ref[0].astype(BF16), preferred_element_type=F32).astype(BF16)


def _memkv(mem, norm_mem, w_ck, w_cv):
    depth = w_ck.shape[0]
    batch = mem.shape[0]
    out = jax.ShapeDtypeStruct((depth, batch, MEM_LEN, CROSS_WIDTH), BF16)
    wspec = pl.BlockSpec((1, D_MODEL, CROSS_WIDTH), lambda l, b: (l, 0, 0))
    ospec = pl.BlockSpec((1, 1, MEM_LEN, CROSS_WIDTH), lambda l, b: (l, b, 0, 0))
    return pl.pallas_call(
        _memkv_kernel,
        grid=(depth, batch),
        in_specs=[pl.BlockSpec((1, MEM_LEN, D_MODEL), lambda l, b: (b, 0, 0)),
                  pl.BlockSpec((1, 1, D_MODEL), lambda l, b: (l, 0, 0)), wspec, wspec],
        out_specs=[ospec, ospec],
        out_shape=[out, out],
        compiler_params=pltpu.CompilerParams(dimension_semantics=("parallel", "parallel"),
                                             vmem_limit_bytes=VMEM_LIMIT),
        name="mem_kv",
    )(mem, norm_mem.reshape(depth, 1, D_MODEL), w_ck, w_cv)


def _cross_kernel(x_ref, nc_ref, wq_ref, km_ref, vm_ref, wo_ref, nf_ref, wrt_ref, brt_ref, u_ref,
                  o_ref, cls_ref, rank_ref, cnt_ref, carry_ref, *, tm):
    @pl.when(pl.program_id(0) == 0)
    def _():
        carry_ref[...] = jnp.zeros_like(carry_ref)

    x = x_ref[...]
    h = _rms(x, nc_ref[...]).astype(BF16)
    q = (jnp.dot(h, wq_ref[...], preferred_element_type=F32) * 0.125).astype(BF16)
    km = km_ref[0]
    vm = vm_ref[0]
    qgrp = lax.broadcasted_iota(jnp.int32, (tm, CROSS_WIDTH), 1) // 64
    vgrp = lax.broadcasted_iota(jnp.int32, (MEM_LEN, CROSS_WIDTH), 1) // 64
    o = jnp.zeros((tm, CROSS_WIDTH), F32)
    for hh in range(4):
        s = _nt_dot(jnp.where(qgrp == hh, q, jnp.zeros_like(q)), km)
        p = jnp.exp(s - jnp.max(s, axis=-1, keepdims=True))
        p = p / jnp.sum(p, axis=-1, keepdims=True)
        o = o + jnp.dot(p.astype(BF16), jnp.where(vgrp == hh, vm, jnp.zeros_like(vm)),
                        preferred_element_type=F32)
    x2 = x + jnp.dot(o.astype(BF16), wo_ref[...], preferred_element_type=F32)
    o_ref[...] = x2

    hf = _rms(x2, nf_ref[...])
    lt = _nt_dot(wrt_ref[...], hf, precision=lax.Precision.HIGHEST) + brt_ref[...]
    gl = [lt[r:r + 1, :] for r in range(4)]
    el = [lt[4 + r:5 + r, :] for r in range(16)]

    def first_argmax(vals):
        mx = functools.reduce(jnp.maximum, vals)
        idx = jnp.full(vals[0].shape, len(vals) - 1, jnp.int32)
        for r in range(len(vals) - 2, -1, -1):
            idx = jnp.where(vals[r] == mx, r, idx)
        return idx

    g = first_argmax(gl)
    es = [jnp.where(g == 0, el[r], jnp.where(g == 1, el[4 + r], jnp.where(g == 2, el[8 + r], el[12 + r])))
          for r in range(4)]
    i1 = first_argmax(es)
    i2 = first_argmax([jnp.where(i1 == r, -jnp.inf, es[r]) for r in range(4)])
    a = jnp.minimum(i1, i2)
    b = jnp.maximum(i1, i2)
    cls = g * 6 + jnp.where(a == 0, b - 1, jnp.where(a == 1, b + 1, 5))

    onehot = lax.broadcasted_iota(jnp.int32, (CLASS_ROWS, tm), 0) == cls
    oh = jnp.where(onehot, 1.0, 0.0)
    before = jnp.dot(oh.astype(BF16), u_ref[...], preferred_element_type=F32) + carry_ref[...]
    rank = jnp.sum(jnp.where(onehot, before, 0.0), axis=0, keepdims=True)
    carry_ref[...] += jnp.sum(oh, axis=1, keepdims=True)
    cls_ref[0] = cls
    rank_ref[0] = rank.astype(jnp.int32)
    cnt_ref[...] = jnp.broadcast_to(carry_ref[...], cnt_ref.shape).astype(jnp.int32)


def _cross_router(x2, nc_g, wq_bf, kmem, vmem, wo_bf, nf_g, wrt_t, brt_col, seq):
    t = x2.shape[0]
    tm = ROW_TILE
    nt = t // tm
    per_b = seq // tm
    u = (lax.broadcasted_iota(jnp.int32, (tm, tm), 0) < lax.broadcasted_iota(jnp.int32, (tm, tm), 1)).astype(BF16)
    row = pl.BlockSpec((tm, D_MODEL), lambda i: (i, 0))
    full = lambda a: pl.BlockSpec(a.shape, lambda i: (0,) * a.ndim)
    mspec = pl.BlockSpec((1, MEM_LEN, CROSS_WIDTH), lambda i: (i // per_b, 0, 0))
    tokspec = pl.BlockSpec((1, 1, tm), lambda i: (i, 0, 0))
    return pl.pallas_call(
        functools.partial(_cross_kernel, tm=tm),
        grid=(nt,),
        in_specs=[row, full(nc_g), full(wq_bf), mspec, mspec, full(wo_bf), full(nf_g), full(wrt_t),
                  full(brt_col), full(u)],
        out_specs=[row, tokspec, tokspec, pl.BlockSpec((CLASS_ROWS, 128), lambda i: (0, 0))],
        out_shape=[jax.ShapeDtypeStruct(x2.shape, F32), jax.ShapeDtypeStruct((nt, 1, tm), jnp.int32),
                   jax.ShapeDtypeStruct((nt, 1, tm), jnp.int32),
                   jax.ShapeDtypeStruct((CLASS_ROWS, 128), jnp.int32)],
        scratch_shapes=[pltpu.VMEM((CLASS_ROWS, 1), F32)],
        compiler_params=pltpu.CompilerParams(dimension_semantics=("arbitrary",), vmem_limit_bytes=VMEM_LIMIT),
        name="cross_router",
    )(x2, nc_g, wq_bf, kmem, vmem, wo_bf, nf_g, wrt_t, brt_col, u)


def _moe_kernel(e1_ref, e2_ref, nv_ref, src_ref,
                x_hbm, nf_ref, wrt_ref, brt_ref, wg1_ref, wu1_ref, wd1_ref, wg2_ref, wu2_ref, wd2_ref,
                nfin_ref, out_hbm, xbuf, obuf, wgc, wuc, wdc, gsem, ssem, *, tile, n_tiles, final):
    k = pl.program_id(0)
    slot = k % 2
    prev = jnp.maximum(k - 1, 0)

    def gather_row(kk, sl, r):
        tok = src_ref[kk * tile + r]
        return pltpu.make_async_copy(x_hbm.at[pl.ds(tok, 1), :], xbuf.at[sl, pl.ds(r, 1), :], gsem.at[sl])

    def scatter_row(kk, sl, r):
        tok = src_ref[kk * tile + r]
        return pltpu.make_async_copy(obuf.at[sl, pl.ds(r, 1), :], out_hbm.at[pl.ds(tok, 1), :], ssem.at[sl])

    def start_gather(kk, sl):
        def body(r, c):
            gather_row(kk, sl, r).start()
            return c
        lax.fori_loop(0, tile, body, 0, unroll=8)

    def wait_gather(sl):
        pltpu.make_async_copy(x_hbm.at[pl.ds(0, tile), :], xbuf.at[sl], gsem.at[sl]).wait()

    def wait_scatter(n, sl):
        n8 = pl.multiple_of((n // 8) * 8, 8)

        @pl.when(n8 > 0)
        def _():
            pltpu.make_async_copy(obuf.at[sl, pl.ds(0, n8), :], out_hbm.at[pl.ds(0, n8), :], ssem.at[sl]).wait()

        def body(r, c):
            pltpu.make_async_copy(obuf.at[sl, pl.ds(0, 1), :], out_hbm.at[pl.ds(0, 1), :], ssem.at[sl]).wait()
            return c
        lax.fori_loop(n8, n, body, 0)

    @pl.when(jnp.logical_and(k == 0, nv_ref[0] > 0))
    def _():
        start_gather(0, 0)

    @pl.when(k >= 2)
    def _():
        wait_scatter(nv_ref[jnp.maximum(k - 2, 0)], slot)

    @pl.when(jnp.logical_and(nv_ref[k] == 0, jnp.logical_and(k > 0, nv_ref[prev] > 0)))
    def _():
        wait_gather(slot)

    @pl.when(nv_ref[k] > 0)
    def _():
        @pl.when(jnp.logical_or(k == 0, e1_ref[k] != e1_ref[prev]))
        def _():
            wgc[0] = wg1_ref[0, 0].astype(BF16)
            wuc[0] = wu1_ref[0, 0].astype(BF16)
            wdc[0] = wd1_ref[0, 0].astype(BF16)

        @pl.when(jnp.logical_or(k == 0, e2_ref[k] != e2_ref[prev]))
        def _():
            wgc[1] = wg2_ref[0, 0].astype(BF16)
            wuc[1] = wu2_ref[0, 0].astype(BF16)
            wdc[1] = wd2_ref[0, 0].astype(BF16)

        wait_gather(slot)
        x = xbuf[slot]
        obuf[slot] = x
        hf = _rms(x, nf_ref[...])
        h = hf.astype(BF16)
        for r in range(tile):
            gather_row(k + 1, 1 - slot, r).start()
        lg = jnp.dot(hf, wrt_ref[...], preferred_element_type=F32, precision=lax.Precision.HIGHEST) + brt_ref[...]
        lane = lax.broadcasted_iota(jnp.int32, lg.shape, 1)
        e1 = e1_ref[k]
        e2 = e2_ref[k]
        glm = jnp.where(lane < N_EXPERT_GROUPS, lg, -jnp.inf)
        eg = jnp.exp(glm - jnp.max(glm, axis=-1, keepdims=True))
        pick = lambda a, idx: jnp.sum(jnp.where(lane == idx, a, 0.0), axis=-1, keepdims=True)
        p_group = pick(eg, e1 // EXPERTS_PER_GROUP) / jnp.sum(eg, axis=-1, keepdims=True)
        la = pick(lg, N_EXPERT_GROUPS + e1)
        lb = pick(lg, N_EXPERT_GROUPS + e2)
        mx = jnp.maximum(la, lb)
        ea = jnp.exp(la - mx)
        eb = jnp.exp(lb - mx)
        wa = ea / (ea + eb) * p_group
        wb = eb / (ea + eb) * p_group

        def expert(s):
            gt = jnp.dot(h, wgc[s], preferred_element_type=F32)
            up = jnp.dot(h, wuc[s], preferred_element_type=F32)
            act = (jax.nn.silu(gt) * up).astype(BF16)
            return jnp.dot(act, wdc[s], preferred_element_type=F32)

        y = obuf[slot] + (wa * expert(0) + wb * expert(1))
        if final:
            y = _rms(y, nfin_ref[...])
        obuf[slot] = y

        n_rows = nv_ref[k]

        def body8(g, c):
            for r in range(8):
                scatter_row(k, slot, g * 8 + r).start()
            return c
        lax.fori_loop(0, n_rows // 8, body8, 0)

        def body(r, c):
            scatter_row(k, slot, r).start()
            return c
        lax.fori_loop((n_rows // 8) * 8, n_rows, body, 0)

    @pl.when(k == n_tiles - 1)
    def _():
        wait_scatter(nv_ref[k], slot)
        wait_scatter(nv_ref[prev], 1 - slot)

        @pl.when(nv_ref[k] > 0)
        def _():
            wait_gather(1 - slot)


def _moe(x2, e1, e2, nv, src, nf_g, wrt, brt_row, w_gate, w_up, w_down, layer, nfin_g, final):
    t = x2.shape[0]
    tile = MOE_TILE
    n_tiles = e1.shape[0]
    full = lambda a: pl.BlockSpec(a.shape, lambda k, *_: (0,) * a.ndim)
    wspec1 = lambda a: pl.BlockSpec((1, 1) + a.shape[2:], lambda k, e1r, e2r, nvr, srcr: (layer, e1r[k], 0, 0))
    wspec2 = lambda a: pl.BlockSpec((1, 1) + a.shape[2:], lambda k, e1r, e2r, nvr, srcr: (layer, e2r[k], 0, 0))
    anyspec = pl.BlockSpec(memory_space=pl.ANY)
    grid_spec = pltpu.PrefetchScalarGridSpec(
        num_scalar_prefetch=4,
        grid=(n_tiles,),
        in_specs=[anyspec, full(nf_g), full(wrt), full(brt_row),
                  wspec1(w_gate), wspec1(w_up), wspec1(w_down),
                  wspec2(w_gate), wspec2(w_up), wspec2(w_down), full(nfin_g)],
        out_specs=anyspec,
        scratch_shapes=[pltpu.VMEM((2, tile, D_MODEL), F32), pltpu.VMEM((2, tile, D_MODEL), F32),
                        pltpu.VMEM((2, D_MODEL, D_EXPERT), BF16), pltpu.VMEM((2, D_MODEL, D_EXPERT), BF16),
                        pltpu.VMEM((2, D_EXPERT, D_MODEL), BF16),
                        pltpu.SemaphoreType.DMA((2,)), pltpu.SemaphoreType.DMA((2,))],
    )
    return pl.pallas_call(
        functools.partial(_moe_kernel, tile=tile, n_tiles=n_tiles, final=final),
        grid_spec=grid_spec,
        out_shape=jax.ShapeDtypeStruct((t, D_MODEL), F32),
        compiler_params=pltpu.CompilerParams(dimension_semantics=("arbitrary",), vmem_limit_bytes=VMEM_LIMIT),
        name="moe_experts",
    )(e1, e2, nv, src, x2, nf_g, wrt, brt_row, w_gate, w_up, w_down, w_gate, w_up, w_down, nfin_g)


def _route_meta(cls, rank, counts, t, tile, n_tiles):
    tiles_per = (counts + tile - 1) // tile
    tile_end = jnp.cumsum(tiles_per)
    tile_start = tile_end - tiles_per
    pos = tile_start[cls] * tile + rank
    src = jnp.zeros(((n_tiles + 1) * tile,), jnp.int32).at[pos].set(jnp.arange(t, dtype=jnp.int32),
                                                                    unique_indices=True)
    kk = jnp.arange(n_tiles, dtype=jnp.int32)
    tcls = jnp.sum((kk[:, None] >= tile_end[None, :]).astype(jnp.int32), axis=1)
    valid = tcls < N_CLASSES
    last_cls = tcls[jnp.maximum(tile_end[-1] - 1, 0)]
    tc = jnp.minimum(jnp.where(valid, tcls, last_cls), N_CLASSES - 1)
    nv = jnp.where(valid, jnp.clip(counts[tc] - (kk - tile_start[tc]) * tile, 0, tile), 0)
    grp = tc // 6
    e1 = grp * EXPERTS_PER_GROUP + jnp.asarray(PAIR_LO, jnp.int32)[tc % 6]
    e2 = grp * EXPERTS_PER_GROUP + jnp.asarray(PAIR_HI, jnp.int32)[tc % 6]
    return e1.astype(jnp.int32), e2.astype(jnp.int32), nv.astype(jnp.int32), src


def _rotary_lane_tables(positions):
    lane = jnp.arange(128, dtype=jnp.int32) % DIFF_QK_DIM
    half = ROPE_DIM // 2
    inv_freq = ROPE_THETA ** (-(2 * (lane % half)).astype(F32) / ROPE_DIM)
    inv_freq = jnp.where(lane < ROPE_DIM, inv_freq, 0.0)
    sign = jnp.where(lane < half, -1.0, 1.0).astype(F32)
    ang = positions.astype(F32).reshape(-1, 1) * inv_freq[None, :]
    return jnp.cos(ang), jnp.sin(ang) * sign[None, :]


def kernel(x, mem, positions, norm_mix, w_in, lam_q1, lam_k1, lam_q2, lam_k2, subln_a, norm_b_out, sgu_norm, w_spatial, b_spatial, norm_c_out, w_out, norm_cross, norm_mem, w_cq, w_ck, w_cv, w_co, norm_ffn, w_group, b_group, w_router, b_router, w_gate, w_up, w_down, norm_final):
    batch, seq, d = x.shape
    depth = w_in.shape[0]
    t = batch * seq
    n_tiles = t // MOE_TILE + N_CLASSES
    x2 = x.reshape(t, d)
    rc, rs = _rotary_lane_tables(positions)
    kmem, vmem = _memkv(mem, norm_mem, w_ck, w_cv)
    row = lambda a: a.reshape(1, -1)
    for l in range(depth):
        lam_init = 0.8 - 0.6 * math.exp(-0.3 * l)
        w_vt = jnp.concatenate([w_in[l][:, 1024:1536], w_in[l][:, 2048:2304]], axis=1).T.astype(BF16)
        aq, ak, avt, bq, bk, bvt, cu, cv = _inproj(x2, row(norm_mix[l]), w_in[l].astype(BF16), w_vt, rc, rs)
        lam_p = jnp.stack([lam_q1[l], lam_k1[l], lam_q2[l], lam_k2[l]])
        ya = _diff_attention(aq, ak, avt, lam_p, row(subln_a[l]), lam_init, batch, seq)
        yb = _stick_breaking(bq, bk, bvt, row(norm_b_out[l]), batch, seq)
        b_full = jnp.repeat(b_spatial[l].T, 64, axis=1)
        x2 = _mixout(x2, ya, yb, cu, cv, row(sgu_norm[l]), w_spatial[l], b_full, row(norm_c_out[l]),
                     w_out[l].astype(BF16))
        wrt = jnp.concatenate([w_group[l], w_router[l],
                               jnp.zeros((d, ROUTER_LANES - N_EXPERT_GROUPS - N_EXPERTS), F32)], axis=1)
        brt = jnp.concatenate([b_group[l], b_router[l],
                               jnp.zeros((ROUTER_LANES - N_EXPERT_GROUPS - N_EXPERTS,), F32)])
        x2, cls, rank, cnt = _cross_router(x2, row(norm_cross[l]), w_cq[l].astype(BF16), kmem[l], vmem[l],
                                           w_co[l].astype(BF16), row(norm_ffn[l]), wrt.T, brt.reshape(-1, 1), seq)
        e1, e2, nv, src = _route_meta(cls.reshape(t), rank.reshape(t), cnt[:N_CLASSES, 0], t, MOE_TILE, n_tiles)
        x2 = _moe(x2, e1, e2, nv, src, row(norm_ffn[l]), wrt, brt.reshape(1, -1), w_gate, w_up, w_down, l,
                  row(norm_final), final=(l == depth - 1))
    return x2.reshape(batch, seq, d)
```

```python
import functools
import math

import jax
import jax.numpy as jnp
from jax import lax
from jax.experimental import pallas as pl
from jax.experimental.pallas import tpu as pltpu

F32 = jnp.float32
BF16 = jnp.bfloat16
EPS = 1e-6

D_MODEL = 1024
N_HEADS_A = 4
DIFF_QK_DIM = 64
N_HEADS_B = 4
SB_DIM = 64
N_GROUPS_C = 4
CHUNK = 128
WIDTH_A = 512
WIDTH_B = 256
WIDTH_C = 256
D_IN = 2816
ROPE_THETA = 500000.0
ROPE_DIM = 16
MEM_LEN = 256
CROSS_WIDTH = 256
N_EXPERT_GROUPS = 4
EXPERTS_PER_GROUP = 4
N_EXPERTS = 16
D_EXPERT = 512
N_CLASSES = 24
CLASS_ROWS = 32
ROUTER_LANES = 128
PAIR_LO = (0, 0, 0, 1, 1, 2)
PAIR_HI = (1, 2, 3, 2, 3, 3)

ROW_TILE = 512
DIFF_TILE = 512
SB_TILE = 256
ATT_CHUNK = 256
SOFTMAX_SLAB = 128
LOG2E = 1.4426950408889634
SB_DEAD_LOG = 110.0
MOE_TILE = 256
VMEM_LIMIT = 56 * 1024 * 1024


def _rms(x, g):
    return x * lax.rsqrt(jnp.mean(x * x, axis=-1, keepdims=True) + EPS) * g


def _group_rms64(x, g):
    m, w = x.shape
    grp = lax.broadcasted_iota(jnp.int32, (m, w), 1) // 64
    sq = x * x
    scale = jnp.zeros_like(x)
    for gi in range(w // 64):
        sel = grp == gi
        ss = jnp.sum(jnp.where(sel, sq, 0.0), axis=-1, keepdims=True)
        scale = jnp.where(sel, lax.rsqrt(ss * (1.0 / 64.0) + EPS), scale)
    return x * scale * g


def _split_bf16(x):
    hi = x.astype(BF16)
    return hi, (x - hi.astype(F32)).astype(BF16)


def _nt_dot(a, b, **kw):
    return lax.dot_general(a, b, (((1,), (1,)), ((), ())), preferred_element_type=F32, **kw)


def _inproj_kernel(x_ref, g_ref, w_ref, wvt_ref, c_ref, s_ref,
                   aq_ref, ak_ref, avt_ref, bq_ref, bk_ref, bvt_ref, cu_ref, cv_ref, *, tm):
    h = _rms(x_ref[...], g_ref[...]).astype(BF16)

    def proj(lo, hi):
        return jnp.dot(h, w_ref[:, lo:hi], preferred_element_type=F32)

    c, s = c_ref[...], s_ref[...]
    first_half = lax.broadcasted_iota(jnp.int32, s.shape, 1) % DIFF_QK_DIM < ROPE_DIM // 2
    s1 = jnp.where(first_half, s, 0.0)
    s2 = s - s1

    def rot(t):
        outs = []
        for hh in range(4):
            th = t[:, hh * 128:(hh + 1) * 128]
            outs.append(th * c + pltpu.roll(th, 120, 1) * s1 + pltpu.roll(th, 8, 1) * s2)
        return jnp.concatenate(outs, axis=1)

    aq_ref[...] = (rot(proj(0, 512)) * (0.125 * LOG2E)).astype(BF16)
    ak_ref[...] = rot(proj(512, 1024)).astype(BF16)
    bq_ref[...] = (proj(1536, 1792) * 0.125).astype(BF16)
    bk_ref[...] = proj(1792, 2048).astype(BF16)
    cu_ref[...] = jax.nn.gelu(proj(2304, 2560))
    cv_ref[...] = jax.nn.gelu(proj(2560, 2816))
    vt = _nt_dot(wvt_ref[...], h).astype(BF16)
    for blk in range(tm // DIFF_TILE):
        avt_ref[blk] = vt[0:WIDTH_A, blk * DIFF_TILE:(blk + 1) * DIFF_TILE]
    for blk in range(tm // SB_TILE):
        bvt_ref[blk] = vt[WIDTH_A:, blk * SB_TILE:(blk + 1) * SB_TILE]


def _inproj(x2, g, w_bf, wvt_bf, rc, rs):
    t = x2.shape[0]
    tm = ROW_TILE
    row = lambda w: pl.BlockSpec((tm, w), lambda i: (i, 0))
    full = lambda a: pl.BlockSpec(a.shape, lambda i: (0,) * a.ndim)
    slab = lambda w, tile: pl.BlockSpec((tm // tile, w, tile), lambda i: (i, 0, 0))
    sds = jax.ShapeDtypeStruct
    outs = [sds((t, 512), BF16), sds((t, 512), BF16), sds((t // DIFF_TILE, WIDTH_A, DIFF_TILE), BF16),
            sds((t, 256), BF16), sds((t, 256), BF16), sds((t // SB_TILE, WIDTH_B, SB_TILE), BF16),
            sds((t, 256), F32), sds((t, 256), F32)]
    return pl.pallas_call(
        functools.partial(_inproj_kernel, tm=tm),
        grid=(t // tm,),
        in_specs=[row(D_MODEL), full(g), full(w_bf), full(wvt_bf), row(128), row(128)],
        out_specs=[row(512), row(512), slab(WIDTH_A, DIFF_TILE), row(256), row(256), slab(WIDTH_B, SB_TILE),
                   row(256), row(256)],
        out_shape=outs,
        compiler_params=pltpu.CompilerParams(dimension_semantics=("parallel",), vmem_limit_bytes=VMEM_LIMIT),
        name="inproj",
    )(x2, g, w_bf, wvt_bf, rc, rs)


def _stack_queries(q_ref, qs_ref, tq):
    q = q_ref[...]
    lane = lax.broadcasted_iota(jnp.int32, (tq, 128), 1)
    zero = jnp.zeros_like(q)
    qs_ref[0:tq, :] = jnp.where(lane < 64, q, zero)
    qs_ref[tq:, :] = jnp.where(lane >= 64, q, zero)


def _fold8(x, op):
    n, w = x.shape
    return op(x.reshape(n // 8, 8, w), axis=0)


def _key_le_query(nk, cw, q0, strict):
    key = lax.broadcasted_iota(jnp.int32, (nk, cw), 0)
    qry = lax.broadcasted_iota(jnp.int32, (nk, cw), 1) + q0
    return key < qry if strict else key <= qry


def _diff_kernel(lam_ref, g_ref, q_ref, k_ref, vt_ref, o_ref, qs_ref, m_ref, l_ref, acc_ref,
                 s_buf, p_buf, a_buf, *, tq, cw, lam_init):
    i = pl.program_id(2)
    chunks = [slice(c * cw, (c + 1) * cw) for c in range(2 * tq // cw)]
    _stack_queries(q_ref, qs_ref, tq)
    m_ref[...] = jnp.full_like(m_ref, -jnp.inf)
    l_ref[...] = jnp.zeros_like(l_ref)
    acc_ref[...] = jnp.zeros_like(acc_ref)
    p_buf[...] = jnp.zeros_like(p_buf)
    a_buf[...] = jnp.ones_like(a_buf)

    def scores(j):
        k = k_ref[pl.ds(pl.multiple_of(j * tq, tq), tq), :]
        return [_nt_dot(k, qs_ref[cs, :]) for cs in chunks]

    def hand_off(s_new):
        for c, cs in enumerate(chunks):
            s_buf[:, cs] = s_new[c]

    def softmax(masked):
        for c, cs in enumerate(chunks):
            q0 = (c * cw) % tq
            nk = q0 + cw if masked else tq
            slabs = [slice(r, r + SOFTMAX_SLAB) for r in range(0, nk, SOFTMAX_SLAB)]

            def load(rs):
                sc = s_buf[rs, cs]
                if masked and rs.stop > q0:
                    keep = _key_le_query(SOFTMAX_SLAB, cw, q0 - rs.start, False)
                    sc = jnp.where(keep, sc, -jnp.inf)
                return sc

            m_old = m_ref[:, cs]
            m8 = functools.reduce(jnp.maximum, [_fold8(load(rs), jnp.max) for rs in slabs])
            m_new = jnp.maximum(m_old, jnp.max(m8, axis=0, keepdims=True))
            alpha = jnp.exp2(m_old - m_new)
            l8 = []
            for rs in slabs:
                p = jnp.exp2(load(rs) - m_new)
                l8.append(_fold8(p, jnp.sum))
                p_buf[rs, cs] = p.astype(BF16)
            l_ref[:, cs] = alpha * l_ref[:, cs] + jnp.sum(functools.reduce(jnp.add, l8), axis=0, keepdims=True)
            m_ref[:, cs] = m_new
            if nk < tq:
                p_buf[nk:, cs] = jnp.zeros((tq - nk, cw), BF16)
            a_buf[:, cs] = alpha

    def values_matmul(j):
        vt = vt_ref[jnp.maximum(j, 0)]
        return [(a_buf[:, cs], jnp.dot(vt, p_buf[:, cs], preferred_element_type=F32)) for cs in chunks]

    def accumulate(scaled):
        for cs, (alpha, pv) in zip(chunks, scaled):
            acc_ref[:, cs] = alpha * acc_ref[:, cs] + pv

    def values(j):
        accumulate(values_matmul(j))

    hand_off(scores(0))

    def body(t, carry):
        s_new = scores(t + 1)
        scaled = values_matmul(t - 1)
        softmax(False)
        accumulate(scaled)
        hand_off(s_new)
        return carry

    lax.fori_loop(0, i, body, 0)
    values(i - 1)
    softmax(True)
    values(i)

    lp = lam_ref[...]
    lam = (jnp.exp(jnp.sum(lp[0:1] * lp[1:2], axis=-1, keepdims=True))
           - jnp.exp(jnp.sum(lp[2:3] * lp[3:4], axis=-1, keepdims=True)) + lam_init)
    o = acc_ref[...] / l_ref[...]
    o = o[:, 0:tq] - lam * o[:, tq:]
    o = o * lax.rsqrt(jnp.mean(o * o, axis=0, keepdims=True) + EPS)
    o_ref[...] = (o.T * g_ref[...] * (1.0 - lam_init)).astype(o_ref.dtype)


def _diff_attention(aq, ak, avt, lam_p, subln_g, lam_init, batch, seq):
    tq = DIFF_TILE
    nq = seq // tq
    kern = functools.partial(_diff_kernel, tq=tq, cw=ATT_CHUNK, lam_init=lam_init)
    qspec = pl.BlockSpec((tq, 128), lambda b, h, i: (b * nq + i, h))
    kspec = pl.BlockSpec((seq, 128), lambda b, h, i: (b, h))
    vspec = pl.BlockSpec((nq, 128, tq), lambda b, h, i: (b, h, 0))
    return pl.pallas_call(
        kern,
        grid=(batch, N_HEADS_A, nq),
        in_specs=[pl.BlockSpec(lam_p.shape, lambda b, h, i: (0, 0)),
                  pl.BlockSpec(subln_g.shape, lambda b, h, i: (0, 0)),
                  qspec, kspec, vspec],
        out_specs=qspec,
        out_shape=jax.ShapeDtypeStruct(aq.shape, BF16),
        scratch_shapes=[pltpu.VMEM((2 * tq, 128), BF16), pltpu.VMEM((1, 2 * tq), F32),
                        pltpu.VMEM((1, 2 * tq), F32), pltpu.VMEM((128, 2 * tq), F32),
                        pltpu.VMEM((tq, 2 * tq), F32), pltpu.VMEM((tq, 2 * tq), BF16),
                        pltpu.VMEM((1, 2 * tq), F32)],
        compiler_params=pltpu.CompilerParams(dimension_semantics=("parallel", "parallel", "arbitrary"),
                                             vmem_limit_bytes=VMEM_LIMIT),
        name="diff_attn",
    )(lam_p, subln_g, aq, ak, avt)


def _sb_kernel(g_ref, u_ref, q_ref, k_ref, vt_ref, o_ref, qs_ref, c_ref, acc_ref,
               z_buf, hi_buf, lo_buf, r_buf, w_buf, lb_ref, flag_ref, *, tq, cw):
    i = pl.program_id(2)
    chunks = [slice(c * cw, (c + 1) * cw) for c in range(2 * tq // cw)]
    _stack_queries(q_ref, qs_ref, tq)
    c_ref[...] = jnp.zeros_like(c_ref)
    acc_ref[...] = jnp.zeros_like(acc_ref)
    w_buf[...] = jnp.zeros_like(w_buf)
    lb_ref[...] = jnp.zeros_like(lb_ref)

    def scores(n):
        k = k_ref[pl.ds(pl.multiple_of((i - n) * tq, tq), tq), :]
        return [_nt_dot(k, qs_ref[cs, :]) for cs in chunks]

    def hand_off(z_new):
        for c, cs in enumerate(chunks):
            z_buf[:, cs] = z_new[c]

    def breaks(masked):
        for c, cs in enumerate(chunks):
            z = z_buf[:, cs]
            sp = jnp.maximum(z, 0.0) + jnp.log(1.0 + jnp.exp2(jnp.abs(z) * -LOG2E))
            prior = c_ref[:, cs]
            rest = z - sp - prior
            if masked:
                strict = _key_le_query(tq, cw, (c * cw) % tq, True)
                sp = jnp.where(strict, sp, 0.0)
                rest = jnp.where(strict, rest, -jnp.inf)
            hi = sp.astype(BF16)
            hi_buf[:, cs] = hi
            lo_buf[:, cs] = (sp - hi.astype(F32)).astype(BF16)
            r_buf[:, cs] = rest
            c_ref[:, cs] = prior + jnp.sum(sp, axis=0, keepdims=True)

    def weights():
        u = u_ref[...]
        for cs in chunks:
            cum = (jnp.dot(u, hi_buf[:, cs], preferred_element_type=F32)
                   + jnp.dot(u, lo_buf[:, cs], preferred_element_type=F32))
            w_buf[:, cs] = jnp.exp(r_buf[:, cs] - cum).astype(BF16)
            lb_ref[:, cs] = lb_ref[:, cs] + cum[0:1]

    def values(n):
        vt = vt_ref[i - jnp.maximum(n, 0)]
        for cs in chunks:
            acc_ref[:, cs] += jnp.dot(vt, w_buf[:, cs], preferred_element_type=F32)

    hand_off(scores(0))
    breaks(True)
    hand_off(scores(jnp.minimum(1, i)))

    flag_ref[0] = 1
    flag_ref[1] = 1

    def body(n, carry):
        @pl.when(flag_ref[0] == 1)
        def _():
            values(n - 2)
            weights()
            breaks(False)
            hand_off(scores(jnp.minimum(n + 1, i)))
            flag_ref[0] = jnp.max(jnp.where(lb_ref[...] <= SB_DEAD_LOG, 1, 0))
            flag_ref[1] = n + 1
        return carry

    lax.fori_loop(1, i + 1, body, 0)
    n_done = flag_ref[1]
    values(n_done - 2)

    @pl.when(flag_ref[0] == 1)
    def _():
        weights()
        values(n_done - 1)

    def head_norm(o):
        return o * lax.rsqrt(jnp.mean(o * o, axis=0, keepdims=True) + EPS)

    o = jnp.concatenate([head_norm(acc_ref[0:64, 0:tq]), head_norm(acc_ref[64:128, tq:])], axis=0)
    o_ref[...] = (o.T * g_ref[...]).astype(o_ref.dtype)


def _stick_breaking(bq, bk, bvt, norm_g, batch, seq):
    tq = SB_TILE
    nq = seq // tq
    u = (lax.broadcasted_iota(jnp.int32, (tq, tq), 1) > lax.broadcasted_iota(jnp.int32, (tq, tq), 0)).astype(BF16)
    kern = functools.partial(_sb_kernel, tq=tq, cw=ATT_CHUNK)
    qspec = pl.BlockSpec((tq, 128), lambda b, h, i: (b * nq + i, h))
    kspec = pl.BlockSpec((seq, 128), lambda b, h, i: (b, h))
    vspec = pl.BlockSpec((nq, 128, tq), lambda b, h, i: (b, h, 0))
    return pl.pallas_call(
        kern,
        grid=(batch, N_HEADS_B // 2, nq),
        in_specs=[pl.BlockSpec((1, 128), lambda b, h, i: (0, h)),
                  pl.BlockSpec(u.shape, lambda b, h, i: (0, 0)),
                  qspec, kspec, vspec],
        out_specs=qspec,
        out_shape=jax.ShapeDtypeStruct(bq.shape, BF16),
        scratch_shapes=[pltpu.VMEM((2 * tq, 128), BF16), pltpu.VMEM((1, 2 * tq), F32),
                        pltpu.VMEM((128, 2 * tq), F32),
                        pltpu.VMEM((tq, 2 * tq), F32), pltpu.VMEM((tq, 2 * tq), BF16),
                        pltpu.VMEM((tq, 2 * tq), BF16), pltpu.VMEM((tq, 2 * tq), F32),
                        pltpu.VMEM((tq, 2 * tq), BF16), pltpu.VMEM((1, 2 * tq), F32),
                        pltpu.SMEM((2,), jnp.int32)],
        compiler_params=pltpu.CompilerParams(dimension_semantics=("parallel", "parallel", "arbitrary"),
                                             vmem_limit_bytes=VMEM_LIMIT),
        name="stick_breaking",
    )(norm_g, u, bq, bk, bvt)


def _mixout_kernel(x_ref, ya_ref, yb_ref, cu_ref, cv_ref, sg_ref, ws_ref, bs_ref, nc_ref, wo_ref, o_ref, *, tm):
    vn = _group_rms64(cv_ref[...], sg_ref[...]).astype(BF16)
    tril = lax.broadcasted_iota(jnp.int32, (CHUNK, CHUNK), 0) >= lax.broadcasted_iota(jnp.int32, (CHUNK, CHUNK), 1)
    ws = [jnp.where(tril, ws_ref[g], 0.0).astype(BF16) for g in range(N_GROUPS_C)]
    grp = lax.broadcasted_iota(jnp.int32, (CHUNK, WIDTH_C), 1) // 64
    bias = bs_ref[...]
    chunks = []
    for n in range(tm // CHUNK):
        vc = vn[n * CHUNK:(n + 1) * CHUNK, :]
        s = jnp.zeros((CHUNK, WIDTH_C), F32)
        for g in range(N_GROUPS_C):
            s = jnp.where(grp == g, jnp.dot(ws[g], vc, preferred_element_type=F32), s)
        chunks.append(s + bias)
    yc = cu_ref[...] * jnp.concatenate(chunks, axis=0)
    yc = _group_rms64(yc, nc_ref[...]).astype(BF16)
    o_ref[...] = (x_ref[...]
                  + jnp.dot(ya_ref[...], wo_ref[0:512, :], preferred_element_type=F32)
                  + jnp.dot(yb_ref[...], wo_ref[512:768, :], preferred_element_type=F32)
                  + jnp.dot(yc, wo_ref[768:1024, :], preferred_element_type=F32))


def _mixout(x2, ya, yb, cu, cv, sgu_g, w_s, b_full, nc_g, wo_bf):
    t = x2.shape[0]
    tm = ROW_TILE
    row = lambda w: pl.BlockSpec((tm, w), lambda i: (i, 0))
    full = lambda a: pl.BlockSpec(a.shape, lambda i: (0,) * a.ndim)
    return pl.pallas_call(
        functools.partial(_mixout_kernel, tm=tm),
        grid=(t // tm,),
        in_specs=[row(D_MODEL), row(512), row(256), row(256), row(256),
                  full(sgu_g), full(w_s), full(b_full), full(nc_g), full(wo_bf)],
        out_specs=row(D_MODEL),
        out_shape=jax.ShapeDtypeStruct(x2.shape, F32),
        compiler_params=pltpu.CompilerParams(dimension_semantics=("parallel",), vmem_limit_bytes=VMEM_LIMIT),
        name="sgu_outproj",
    )(x2, ya, yb, cu, cv, sgu_g, w_s, b_full, nc_g, wo_bf)


def _memkv_kernel(m_ref, g_ref, wk_ref, wv_ref, k_ref, v_ref):
    h = _rms(m_ref[0], g_ref[0]).astype(BF16)
    k_ref[0, 0] = jnp.dot(h, wk_ref[0].astype(BF16), preferred_element_type=F32).astype(BF16)
    v_ref[0, 0] = jnp.dot(h, wv_ref[0].astype(BF16), preferred_element_type=F32).astype(BF16)


def _memkv(mem, norm_mem, w_ck, w_cv):
    depth = w_ck.shape[0]
    batch = mem.shape[0]
    out = jax.ShapeDtypeStruct((depth, batch, MEM_LEN, CROSS_WIDTH), BF16)
    wspec = pl.BlockSpec((1, D_MODEL, CROSS_WIDTH), lambda l, b: (l, 0, 0))
    ospec = pl.BlockSpec((1, 1, MEM_LEN, CROSS_WIDTH), lambda l, b: (l, b, 0, 0))
    return pl.pallas_call(
        _memkv_kernel,
        grid=(depth, batch),
        in_specs=[pl.BlockSpec((1, MEM_LEN, D_MODEL), lambda l, b: (b, 0, 0)),
                  pl.BlockSpec((1, 1, D_MODEL), lambda l, b: (l, 0, 0)), wspec, wspec],
        out_specs=[ospec, ospec],
        out_shape=[out, out],
        compiler_params=pltpu.CompilerParams(dimension_semantics=("parallel", "parallel"),
                                             vmem_limit_bytes=VMEM_LIMIT),
        name="mem_kv",
    )(mem, norm_mem.reshape(depth, 1, D_MODEL), w_ck, w_cv)


def _cross_kernel(x_ref, nc_ref, wq_ref, km_ref, vm_ref, wo_ref, nf_ref, wrt_ref, brt_ref, u_ref,
                  o_ref, cls_ref, rank_ref, cnt_ref, carry_ref, *, tm):
    @pl.when(pl.program_id(0) == 0)
    def _():
        carry_ref[...] = jnp.zeros_like(carry_ref)

    x = x_ref[...]
    h = _rms(x, nc_ref[...]).astype(BF16)
    q = (jnp.dot(h, wq_ref[...], preferred_element_type=F32) * 0.125).astype(BF16)
    km = km_ref[0]
    vm = vm_ref[0]
    qgrp = lax.broadcasted_iota(jnp.int32, (tm, CROSS_WIDTH), 1) // 64
    vgrp = lax.broadcasted_iota(jnp.int32, (MEM_LEN, CROSS_WIDTH), 1) // 64
    o = jnp.zeros((tm, CROSS_WIDTH), F32)
    for hh in range(4):
        s = _nt_dot(jnp.where(qgrp == hh, q, jnp.zeros_like(q)), km)
        p = jnp.exp(s - jnp.max(s, axis=-1, keepdims=True))
        p = p / jnp.sum(p, axis=-1, keepdims=True)
        o = o + jnp.dot(p.astype(BF16), jnp.where(vgrp == hh, vm, jnp.zeros_like(vm)),
                        preferred_element_type=F32)
    x2 = x + jnp.dot(o.astype(BF16), wo_ref[...], preferred_element_type=F32)
    o_ref[...] = x2

    hf = _rms(x2, nf_ref[...])
    w_hi, w_lo = _split_bf16(wrt_ref[...])
    h_hi, h_lo = _split_bf16(hf)
    lt = (_nt_dot(w_hi, h_hi) + _nt_dot(w_hi, h_lo) + _nt_dot(w_lo, h_hi)) + brt_ref[...]
    gl = [lt[r:r + 1, :] for r in range(4)]
    el = [lt[4 + r:5 + r, :] for r in range(16)]

    def first_argmax(vals):
        mx = functools.reduce(jnp.maximum, vals)
        idx = jnp.full(vals[0].shape, len(vals) - 1, jnp.int32)
        for r in range(len(vals) - 2, -1, -1):
            idx = jnp.where(vals[r] == mx, r, idx)
        return idx

    g = first_argmax(gl)
    es = [jnp.where(g == 0, el[r], jnp.where(g == 1, el[4 + r], jnp.where(g == 2, el[8 + r], el[12 + r])))
          for r in range(4)]
    i1 = first_argmax(es)
    i2 = first_argmax([jnp.where(i1 == r, -jnp.inf, es[r]) for r in range(4)])
    a = jnp.minimum(i1, i2)
    b = jnp.maximum(i1, i2)
    cls = g * 6 + jnp.where(a == 0, b - 1, jnp.where(a == 1, b + 1, 5))

    onehot = lax.broadcasted_iota(jnp.int32, (CLASS_ROWS, tm), 0) == cls
    oh = jnp.where(onehot, 1.0, 0.0)
    before = jnp.dot(oh.astype(BF16), u_ref[...], preferred_element_type=F32) + carry_ref[...]
    rank = jnp.sum(jnp.where(onehot, before, 0.0), axis=0, keepdims=True)
    carry_ref[...] += jnp.sum(oh, axis=1, keepdims=True)
    cls_ref[0] = cls
    rank_ref[0] = rank.astype(jnp.int32)
    cnt_ref[...] = jnp.broadcast_to(carry_ref[...], cnt_ref.shape).astype(jnp.int32)


def _cross_router(x2, nc_g, wq_bf, kmem, vmem, wo_bf, nf_g, wrt_t, brt_col, seq):
    t = x2.shape[0]
    tm = ROW_TILE
    nt = t // tm
    per_b = seq // tm
    u = (lax.broadcasted_iota(jnp.int32, (tm, tm), 0) < lax.broadcasted_iota(jnp.int32, (tm, tm), 1)).astype(BF16)
    row = pl.BlockSpec((tm, D_MODEL), lambda i: (i, 0))
    full = lambda a: pl.BlockSpec(a.shape, lambda i: (0,) * a.ndim)
    mspec = pl.BlockSpec((1, MEM_LEN, CROSS_WIDTH), lambda i: (i // per_b, 0, 0))
    tokspec = pl.BlockSpec((1, 1, tm), lambda i: (i, 0, 0))
    return pl.pallas_call(
        functools.partial(_cross_kernel, tm=tm),
        grid=(nt,),
        in_specs=[row, full(nc_g), full(wq_bf), mspec, mspec, full(wo_bf), full(nf_g), full(wrt_t),
                  full(brt_col), full(u)],
        out_specs=[row, tokspec, tokspec, pl.BlockSpec((CLASS_ROWS, 128), lambda i: (0, 0))],
        out_shape=[jax.ShapeDtypeStruct(x2.shape, F32), jax.ShapeDtypeStruct((nt, 1, tm), jnp.int32),
                   jax.ShapeDtypeStruct((nt, 1, tm), jnp.int32),
                   jax.ShapeDtypeStruct((CLASS_ROWS, 128), jnp.int32)],
        scratch_shapes=[pltpu.VMEM((CLASS_ROWS, 1), F32)],
        compiler_params=pltpu.CompilerParams(dimension_semantics=("arbitrary",), vmem_limit_bytes=VMEM_LIMIT),
        name="cross_router",
    )(x2, nc_g, wq_bf, kmem, vmem, wo_bf, nf_g, wrt_t, brt_col, u)


def _moe_kernel(e1_ref, e2_ref, nv_ref, src_ref,
                x_hbm, nf_ref, wrt_ref, brt_ref, wg1_ref, wu1_ref, wd1_ref, wg2_ref, wu2_ref, wd2_ref,
                nfin_ref, out_hbm, xbuf, obuf, wgc, wuc, wdc, gsem, ssem, *, tile, n_tiles, final):
    k = pl.program_id(0)
    slot = k % 2
    prev = jnp.maximum(k - 1, 0)

    def gather_row(kk, sl, r):
        tok = src_ref[kk * tile + r]
        return pltpu.make_async_copy(x_hbm.at[pl.ds(tok, 1), :], xbuf.at[sl, pl.ds(r, 1), :], gsem.at[sl])

    def scatter_row(kk, sl, r):
        tok = src_ref[kk * tile + r]
        return pltpu.make_async_copy(obuf.at[sl, pl.ds(r, 1), :], out_hbm.at[pl.ds(tok, 1), :], ssem.at[sl])

    def start_gather(kk, sl):
        def body(r, c):
            gather_row(kk, sl, r).start()
            return c
        lax.fori_loop(0, tile, body, 0, unroll=8)

    def wait_gather(sl):
        pltpu.make_async_copy(x_hbm.at[pl.ds(0, tile), :], xbuf.at[sl], gsem.at[sl]).wait()

    def wait_scatter(n, sl):
        n8 = pl.multiple_of((n // 8) * 8, 8)

        @pl.when(n8 > 0)
        def _():
            pltpu.make_async_copy(obuf.at[sl, pl.ds(0, n8), :], out_hbm.at[pl.ds(0, n8), :], ssem.at[sl]).wait()

        def body(r, c):
            pltpu.make_async_copy(obuf.at[sl, pl.ds(0, 1), :], out_hbm.at[pl.ds(0, 1), :], ssem.at[sl]).wait()
            return c
        lax.fori_loop(n8, n, body, 0)

    @pl.when(jnp.logical_and(k == 0, nv_ref[0] > 0))
    def _():
        start_gather(0, 0)

    @pl.when(k >= 2)
    def _():
        wait_scatter(nv_ref[jnp.maximum(k - 2, 0)], slot)

    nxt = jnp.minimum(k + 1, n_tiles - 1)

    @pl.when(jnp.logical_and(k + 1 < n_tiles, nv_ref[nxt] > 0))
    def _():
        start_gather(k + 1, 1 - slot)

    @pl.when(nv_ref[k] > 0)
    def _():
        @pl.when(jnp.logical_or(k == 0, e1_ref[k] != e1_ref[prev]))
        def _():
            wgc[0] = wg1_ref[0, 0].astype(BF16)
            wuc[0] = wu1_ref[0, 0].astype(BF16)
            wdc[0] = wd1_ref[0, 0].astype(BF16)

        @pl.when(jnp.logical_or(k == 0, e2_ref[k] != e2_ref[prev]))
        def _():
            wgc[1] = wg2_ref[0, 0].astype(BF16)
            wuc[1] = wu2_ref[0, 0].astype(BF16)
            wdc[1] = wd2_ref[0, 0].astype(BF16)

        wait_gather(slot)
        x = xbuf[slot]
        hf = _rms(x, nf_ref[...])
        h = hf.astype(BF16)
        w_hi, w_lo = _split_bf16(wrt_ref[...])
        h_lo = (hf - h.astype(F32)).astype(BF16)
        lg = (jnp.dot(h, w_hi, preferred_element_type=F32) + jnp.dot(h, w_lo, preferred_element_type=F32)
              + jnp.dot(h_lo, w_hi, preferred_element_type=F32)) + brt_ref[...]
        lane = lax.broadcasted_iota(jnp.int32, lg.shape, 1)
        e1 = e1_ref[k]
        e2 = e2_ref[k]
        glm = jnp.where(lane < N_EXPERT_GROUPS, lg, -jnp.inf)
        eg = jnp.exp(glm - jnp.max(glm, axis=-1, keepdims=True))
        pick = lambda a, idx: jnp.sum(jnp.where(lane == idx, a, 0.0), axis=-1, keepdims=True)
        p_group = pick(eg, e1 // EXPERTS_PER_GROUP) / jnp.sum(eg, axis=-1, keepdims=True)
        la = pick(lg, N_EXPERT_GROUPS + e1)
        lb = pick(lg, N_EXPERT_GROUPS + e2)
        mx = jnp.maximum(la, lb)
        ea = jnp.exp(la - mx)
        eb = jnp.exp(lb - mx)
        wa = ea / (ea + eb) * p_group
        wb = eb / (ea + eb) * p_group

        def expert(s):
            gt = jnp.dot(h, wgc[s], preferred_element_type=F32)
            up = jnp.dot(h, wuc[s], preferred_element_type=F32)
            act = (jax.nn.silu(gt) * up).astype(BF16)
            return jnp.dot(act, wdc[s], preferred_element_type=F32)

        y = x + (wa * expert(0) + wb * expert(1))
        if final:
            y = _rms(y, nfin_ref[...])
        obuf[slot] = y

        n_rows = nv_ref[k]

        def body8(g, c):
            for r in range(8):
                scatter_row(k, slot, g * 8 + r).start()
            return c
        lax.fori_loop(0, n_rows // 8, body8, 0)

        def body(r, c):
            scatter_row(k, slot, r).start()
            return c
        lax.fori_loop((n_rows // 8) * 8, n_rows, body, 0)

    @pl.when(k == n_tiles - 1)
    def _():
        wait_scatter(nv_ref[k], slot)
        wait_scatter(nv_ref[prev], 1 - slot)


def _moe(x2, e1, e2, nv, src, nf_g, wrt, brt_row, w_gate, w_up, w_down, layer, nfin_g, final):
    t = x2.shape[0]
    tile = MOE_TILE
    n_tiles = e1.shape[0]
    full = lambda a: pl.BlockSpec(a.shape, lambda k, *_: (0,) * a.ndim)
    wspec1 = lambda a: pl.BlockSpec((1, 1) + a.shape[2:], lambda k, e1r, e2r, nvr, srcr: (layer, e1r[k], 0, 0))
    wspec2 = lambda a: pl.BlockSpec((1, 1) + a.shape[2:], lambda k, e1r, e2r, nvr, srcr: (layer, e2r[k], 0, 0))
    anyspec = pl.BlockSpec(memory_space=pl.ANY)
    grid_spec = pltpu.PrefetchScalarGridSpec(
        num_scalar_prefetch=4,
        grid=(n_tiles,),
        in_specs=[anyspec, full(nf_g), full(wrt), full(brt_row),
                  wspec1(w_gate), wspec1(w_up), wspec1(w_down),
                  wspec2(w_gate), wspec2(w_up), wspec2(w_down), full(nfin_g)],
        out_specs=anyspec,
        scratch_shapes=[pltpu.VMEM((2, tile, D_MODEL), F32), pltpu.VMEM((2, tile, D_MODEL), F32),
                        pltpu.VMEM((2, D_MODEL, D_EXPERT), BF16), pltpu.VMEM((2, D_MODEL, D_EXPERT), BF16),
                        pltpu.VMEM((2, D_EXPERT, D_MODEL), BF16),
                        pltpu.SemaphoreType.DMA((2,)), pltpu.SemaphoreType.DMA((2,))],
    )
    return pl.pallas_call(
        functools.partial(_moe_kernel, tile=tile, n_tiles=n_tiles, final=final),
        grid_spec=grid_spec,
        out_shape=jax.ShapeDtypeStruct((t, D_MODEL), F32),
        compiler_params=pltpu.CompilerParams(dimension_semantics=("arbitrary",), vmem_limit_bytes=VMEM_LIMIT),
        name="moe_experts",
    )(e1, e2, nv, src, x2, nf_g, wrt, brt_row, w_gate, w_up, w_down, w_gate, w_up, w_down, nfin_g)


def _route_meta(cls, rank, counts, t, tile, n_tiles):
    tiles_per = (counts + tile - 1) // tile
    tile_end = jnp.cumsum(tiles_per)
    tile_start = tile_end - tiles_per
    pos = tile_start[cls] * tile + rank
    src = jnp.zeros((n_tiles * tile,), jnp.int32).at[pos].set(jnp.arange(t, dtype=jnp.int32), unique_indices=True)
    kk = jnp.arange(n_tiles, dtype=jnp.int32)
    tcls = jnp.sum((kk[:, None] >= tile_end[None, :]).astype(jnp.int32), axis=1)
    valid = tcls < N_CLASSES
    last_cls = tcls[jnp.maximum(tile_end[-1] - 1, 0)]
    tc = jnp.minimum(jnp.where(valid, tcls, last_cls), N_CLASSES - 1)
    nv = jnp.where(valid, jnp.clip(counts[tc] - (kk - tile_start[tc]) * tile, 0, tile), 0)
    grp = tc // 6
    e1 = grp * EXPERTS_PER_GROUP + jnp.asarray(PAIR_LO, jnp.int32)[tc % 6]
    e2 = grp * EXPERTS_PER_GROUP + jnp.asarray(PAIR_HI, jnp.int32)[tc % 6]
    return e1.astype(jnp.int32), e2.astype(jnp.int32), nv.astype(jnp.int32), src


def _rotary_lane_tables(positions):
    lane = jnp.arange(128, dtype=jnp.int32) % DIFF_QK_DIM
    half = ROPE_DIM // 2
    inv_freq = ROPE_THETA ** (-(2 * (lane % half)).astype(F32) / ROPE_DIM)
    inv_freq = jnp.where(lane < ROPE_DIM, inv_freq, 0.0)
    sign = jnp.where(lane < half, -1.0, 1.0).astype(F32)
    ang = positions.astype(F32).reshape(-1, 1) * inv_freq[None, :]
    return jnp.cos(ang), jnp.sin(ang) * sign[None, :]


def kernel(x, mem, positions, norm_mix, w_in, lam_q1, lam_k1, lam_q2, lam_k2, subln_a, norm_b_out, sgu_norm, w_spatial, b_spatial, norm_c_out, w_out, norm_cross, norm_mem, w_cq, w_ck, w_cv, w_co, norm_ffn, w_group, b_group, w_router, b_router, w_gate, w_up, w_down, norm_final):
    batch, seq, d = x.shape
    depth = w_in.shape[0]
    t = batch * seq
    n_tiles = t // MOE_TILE + N_CLASSES
    x2 = x.reshape(t, d)
    rc, rs = _rotary_lane_tables(positions)
    kmem, vmem = _memkv(mem, norm_mem, w_ck, w_cv)
    row = lambda a: a.reshape(1, -1)
    for l in range(depth):
        lam_init = 0.8 - 0.6 * math.exp(-0.3 * l)
        w_vt = jnp.concatenate([w_in[l][:, 1024:1536], w_in[l][:, 2048:2304]], axis=1).T.astype(BF16)
        aq, ak, avt, bq, bk, bvt, cu, cv = _inproj(x2, row(norm_mix[l]), w_in[l].astype(BF16), w_vt, rc, rs)
        lam_p = jnp.stack([lam_q1[l], lam_k1[l], lam_q2[l], lam_k2[l]])
        ya = _diff_attention(aq, ak, avt, lam_p, row(subln_a[l]), lam_init, batch, seq)
        yb = _stick_breaking(bq, bk, bvt, row(norm_b_out[l]), batch, seq)
        b_full = jnp.repeat(b_spatial[l].T, 64, axis=1)
        x2 = _mixout(x2, ya, yb, cu, cv, row(sgu_norm[l]), w_spatial[l], b_full, row(norm_c_out[l]),
                     w_out[l].astype(BF16))
        wrt = jnp.concatenate([w_group[l], w_router[l],
                               jnp.zeros((d, ROUTER_LANES - N_EXPERT_GROUPS - N_EXPERTS), F32)], axis=1)
        brt = jnp.concatenate([b_group[l], b_router[l],
                               jnp.zeros((ROUTER_LANES - N_EXPERT_GROUPS - N_EXPERTS,), F32)])
        x2, cls, rank, cnt = _cross_router(x2, row(norm_cross[l]), w_cq[l].astype(BF16), kmem[l], vmem[l],
                                           w_co[l].astype(BF16), row(norm_ffn[l]), wrt.T, brt.reshape(-1, 1), seq)
        e1, e2, nv, src = _route_meta(cls.reshape(t), rank.reshape(t), cnt[:N_CLASSES, 0], t, MOE_TILE, n_tiles)
        x2 = _moe(x2, e1, e2, nv, src, row(norm_ffn[l]), wrt, brt.reshape(1, -1), w_gate, w_up, w_down, l,
                  row(norm_final), final=(l == depth - 1))
    return x2.reshape(batch, seq, d)
```

```python
import functools
import math

import jax
import jax.numpy as jnp
from jax import lax
from jax.experimental import pallas as pl
from jax.experimental.pallas import tpu as pltpu

F32 = jnp.float32
BF16 = jnp.bfloat16
EPS = 1e-6

D_MODEL = 1024
N_HEADS_A = 4
DIFF_QK_DIM = 64
N_HEADS_B = 4
SB_DIM = 64
N_GROUPS_C = 4
CHUNK = 128
WIDTH_A = 512
WIDTH_B = 256
WIDTH_C = 256
D_IN = 2816
ROPE_THETA = 500000.0
ROPE_DIM = 16
MEM_LEN = 256
CROSS_WIDTH = 256
N_EXPERT_GROUPS = 4
EXPERTS_PER_GROUP = 4
N_EXPERTS = 16
D_EXPERT = 512
N_CLASSES = 24
CLASS_ROWS = 32
ROUTER_LANES = 128
PAIR_LO = (0, 0, 0, 1, 1, 2)
PAIR_HI = (1, 2, 3, 2, 3, 3)

ROW_TILE = 512
DIFF_TILE = 512
SB_TILE = 256
ATT_CHUNK = 256
SOFTMAX_SLAB = 128
LOG2E = 1.4426950408889634
SB_DEAD_LOG = 110.0
MOE_TILE = 256
VMEM_LIMIT = 56 * 1024 * 1024


def _rms(x, g):
    return x * lax.rsqrt(jnp.mean(x * x, axis=-1, keepdims=True) + EPS) * g


def _group_rms64(x, g):
    m, w = x.shape
    grp = lax.broadcasted_iota(jnp.int32, (m, w), 1) // 64
    sq = x * x
    scale = jnp.zeros_like(x)
    for gi in range(w // 64):
        sel = grp == gi
        ss = jnp.sum(jnp.where(sel, sq, 0.0), axis=-1, keepdims=True)
        scale = jnp.where(sel, lax.rsqrt(ss * (1.0 / 64.0) + EPS), scale)
    return x * scale * g


def _split_bf16(x):
    hi = x.astype(BF16)
    return hi, (x - hi.astype(F32)).astype(BF16)


def _nt_dot(a, b, **kw):
    return lax.dot_general(a, b, (((1,), (1,)), ((), ())), preferred_element_type=F32, **kw)


def _inproj_kernel(x_ref, g_ref, w_ref, wvt_ref, c_ref, s_ref,
                   aq_ref, ak_ref, avt_ref, bq_ref, bk_ref, bvt_ref, cu_ref, cv_ref, *, tm):
    h = _rms(x_ref[...], g_ref[...]).astype(BF16)

    def proj(lo, hi):
        return jnp.dot(h, w_ref[:, lo:hi], preferred_element_type=F32)

    c, s = c_ref[...], s_ref[...]
    first_half = lax.broadcasted_iota(jnp.int32, s.shape, 1) % DIFF_QK_DIM < ROPE_DIM // 2
    s1 = jnp.where(first_half, s, 0.0)
    s2 = s - s1

    def rot(t):
        outs = []
        for hh in range(4):
            th = t[:, hh * 128:(hh + 1) * 128]
            outs.append(th * c + pltpu.roll(th, 120, 1) * s1 + pltpu.roll(th, 8, 1) * s2)
        return jnp.concatenate(outs, axis=1)

    aq_ref[...] = (rot(proj(0, 512)) * (0.125 * LOG2E)).astype(BF16)
    ak_ref[...] = rot(proj(512, 1024)).astype(BF16)
    bq_ref[...] = (proj(1536, 1792) * 0.125).astype(BF16)
    bk_ref[...] = proj(1792, 2048).astype(BF16)
    cu_ref[...] = jax.nn.gelu(proj(2304, 2560))
    cv_ref[...] = jax.nn.gelu(proj(2560, 2816))
    vt = _nt_dot(wvt_ref[...], h).astype(BF16)
    for blk in range(tm // DIFF_TILE):
        avt_ref[blk] = vt[0:WIDTH_A, blk * DIFF_TILE:(blk + 1) * DIFF_TILE]
    for blk in range(tm // SB_TILE):
        bvt_ref[blk] = vt[WIDTH_A:, blk * SB_TILE:(blk + 1) * SB_TILE]


def _inproj(x2, g, w_bf, wvt_bf, rc, rs):
    t = x2.shape[0]
    tm = ROW_TILE
    row = lambda w: pl.BlockSpec((tm, w), lambda i: (i, 0))
    full = lambda a: pl.BlockSpec(a.shape, lambda i: (0,) * a.ndim)
    slab = lambda w, tile: pl.BlockSpec((tm // tile, w, tile), lambda i: (i, 0, 0))
    sds = jax.ShapeDtypeStruct
    outs = [sds((t, 512), BF16), sds((t, 512), BF16), sds((t // DIFF_TILE, WIDTH_A, DIFF_TILE), BF16),
            sds((t, 256), BF16), sds((t, 256), BF16), sds((t // SB_TILE, WIDTH_B, SB_TILE), BF16),
            sds((t, 256), F32), sds((t, 256), F32)]
    return pl.pallas_call(
        functools.partial(_inproj_kernel, tm=tm),
        grid=(t // tm,),
        in_specs=[row(D_MODEL), full(g), full(w_bf), full(wvt_bf), row(128), row(128)],
        out_specs=[row(512), row(512), slab(WIDTH_A, DIFF_TILE), row(256), row(256), slab(WIDTH_B, SB_TILE),
                   row(256), row(256)],
        out_shape=outs,
        compiler_params=pltpu.CompilerParams(dimension_semantics=("parallel",), vmem_limit_bytes=VMEM_LIMIT),
        name="inproj",
    )(x2, g, w_bf, wvt_bf, rc, rs)


def _stack_queries(q_ref, qs_ref, tq):
    q = q_ref[...]
    lane = lax.broadcasted_iota(jnp.int32, (tq, 128), 1)
    zero = jnp.zeros_like(q)
    qs_ref[0:tq, :] = jnp.where(lane < 64, q, zero)
    qs_ref[tq:, :] = jnp.where(lane >= 64, q, zero)


def _fold8(x, op):
    n, w = x.shape
    return op(x.reshape(n // 8, 8, w), axis=0)


def _key_le_query(nk, cw, q0, strict):
    key = lax.broadcasted_iota(jnp.int32, (nk, cw), 0)
    qry = lax.broadcasted_iota(jnp.int32, (nk, cw), 1) + q0
    return key < qry if strict else key <= qry


def _diff_kernel(lam_ref, g_ref, q_ref, k_ref, vt_ref, o_ref, qs_ref, m_ref, l_ref, acc_ref,
                 s_buf, p_buf, a_buf, *, tq, cw, lam_init):
    i = pl.program_id(2)
    chunks = [slice(c * cw, (c + 1) * cw) for c in range(2 * tq // cw)]
    _stack_queries(q_ref, qs_ref, tq)
    m_ref[...] = jnp.full_like(m_ref, -jnp.inf)
    l_ref[...] = jnp.zeros_like(l_ref)
    acc_ref[...] = jnp.zeros_like(acc_ref)
    p_buf[...] = jnp.zeros_like(p_buf)
    a_buf[...] = jnp.ones_like(a_buf)

    def scores(j):
        k = k_ref[pl.ds(pl.multiple_of(j * tq, tq), tq), :]
        return [_nt_dot(k, qs_ref[cs, :]) for cs in chunks]

    def hand_off(s_new):
        for c, cs in enumerate(chunks):
            s_buf[:, cs] = s_new[c]

    def softmax(masked):
        for c, cs in enumerate(chunks):
            q0 = (c * cw) % tq
            nk = q0 + cw if masked else tq
            slabs = [slice(r, r + SOFTMAX_SLAB) for r in range(0, nk, SOFTMAX_SLAB)]

            def load(rs):
                sc = s_buf[rs, cs]
                if masked and rs.stop > q0:
                    keep = _key_le_query(SOFTMAX_SLAB, cw, q0 - rs.start, False)
                    sc = jnp.where(keep, sc, -jnp.inf)
                return sc

            m_old = m_ref[:, cs]
            m8 = functools.reduce(jnp.maximum, [_fold8(load(rs), jnp.max) for rs in slabs])
            m_new = jnp.maximum(m_old, jnp.max(m8, axis=0, keepdims=True))
            alpha = jnp.exp2(m_old - m_new)
            l8 = []
            for rs in slabs:
                p = jnp.exp2(load(rs) - m_new)
                l8.append(_fold8(p, jnp.sum))
                p_buf[rs, cs] = p.astype(BF16)
            l_ref[:, cs] = alpha * l_ref[:, cs] + jnp.sum(functools.reduce(jnp.add, l8), axis=0, keepdims=True)
            m_ref[:, cs] = m_new
            if nk < tq:
                p_buf[nk:, cs] = jnp.zeros((tq - nk, cw), BF16)
            a_buf[:, cs] = alpha

    def values_matmul(j):
        vt = vt_ref[jnp.maximum(j, 0)]
        return [(a_buf[:, cs], jnp.dot(vt, p_buf[:, cs], preferred_element_type=F32)) for cs in chunks]

    def accumulate(scaled):
        for cs, (alpha, pv) in zip(chunks, scaled):
            acc_ref[:, cs] = alpha * acc_ref[:, cs] + pv

    def values(j):
        accumulate(values_matmul(j))

    hand_off(scores(0))

    def body(t, carry):
        s_new = scores(t + 1)
        scaled = values_matmul(t - 1)
        softmax(False)
        accumulate(scaled)
        hand_off(s_new)
        return carry

    lax.fori_loop(0, i, body, 0)
    values(i - 1)
    softmax(True)
    values(i)

    lp = lam_ref[...]
    lam = (jnp.exp(jnp.sum(lp[0:1] * lp[1:2], axis=-1, keepdims=True))
           - jnp.exp(jnp.sum(lp[2:3] * lp[3:4], axis=-1, keepdims=True)) + lam_init)
    o = acc_ref[...] / l_ref[...]
    o = o[:, 0:tq] - lam * o[:, tq:]
    o = o * lax.rsqrt(jnp.mean(o * o, axis=0, keepdims=True) + EPS)
    o_ref[...] = (o.T * g_ref[...] * (1.0 - lam_init)).astype(o_ref.dtype)


def _diff_attention(aq, ak, avt, lam_p, subln_g, lam_init, batch, seq):
    tq = DIFF_TILE
    nq = seq // tq
    kern = functools.partial(_diff_kernel, tq=tq, cw=ATT_CHUNK, lam_init=lam_init)
    qspec = pl.BlockSpec((tq, 128), lambda b, h, i: (b * nq + i, h))
    kspec = pl.BlockSpec((seq, 128), lambda b, h, i: (b, h))
    vspec = pl.BlockSpec((nq, 128, tq), lambda b, h, i: (b, h, 0))
    return pl.pallas_call(
        kern,
        grid=(batch, N_HEADS_A, nq),
        in_specs=[pl.BlockSpec(lam_p.shape, lambda b, h, i: (0, 0)),
                  pl.BlockSpec(subln_g.shape, lambda b, h, i: (0, 0)),
                  qspec, kspec, vspec],
        out_specs=qspec,
        out_shape=jax.ShapeDtypeStruct(aq.shape, BF16),
        scratch_shapes=[pltpu.VMEM((2 * tq, 128), BF16), pltpu.VMEM((1, 2 * tq), F32),
                        pltpu.VMEM((1, 2 * tq), F32), pltpu.VMEM((128, 2 * tq), F32),
                        pltpu.VMEM((tq, 2 * tq), F32), pltpu.VMEM((tq, 2 * tq), BF16),
                        pltpu.VMEM((1, 2 * tq), F32)],
        compiler_params=pltpu.CompilerParams(dimension_semantics=("parallel", "parallel", "arbitrary"),
                                             vmem_limit_bytes=VMEM_LIMIT),
        name="diff_attn",
    )(lam_p, subln_g, aq, ak, avt)


def _sb_kernel(g_ref, u_ref, q_ref, k_ref, vt_ref, o_ref, qs_ref, c_ref, acc_ref,
               z_buf, hi_buf, lo_buf, r_buf, w_buf, lb_ref, flag_ref, *, tq, cw):
    i = pl.program_id(2)
    chunks = [slice(c * cw, (c + 1) * cw) for c in range(2 * tq // cw)]
    _stack_queries(q_ref, qs_ref, tq)
    c_ref[...] = jnp.zeros_like(c_ref)
    acc_ref[...] = jnp.zeros_like(acc_ref)
    w_buf[...] = jnp.zeros_like(w_buf)
    lb_ref[...] = jnp.zeros_like(lb_ref)

    def scores(n):
        k = k_ref[pl.ds(pl.multiple_of((i - n) * tq, tq), tq), :]
        return [_nt_dot(k, qs_ref[cs, :]) for cs in chunks]

    def hand_off(z_new):
        for c, cs in enumerate(chunks):
            z_buf[:, cs] = z_new[c]

    def breaks(masked):
        for c, cs in enumerate(chunks):
            z = z_buf[:, cs]
            sp = jnp.maximum(z, 0.0) + jnp.log(1.0 + jnp.exp2(jnp.abs(z) * -LOG2E))
            prior = c_ref[:, cs]
            rest = z - sp - prior
            if masked:
                strict = _key_le_query(tq, cw, (c * cw) % tq, True)
                sp = jnp.where(strict, sp, 0.0)
                rest = jnp.where(strict, rest, -jnp.inf)
            hi = sp.astype(BF16)
            hi_buf[:, cs] = hi
            lo_buf[:, cs] = (sp - hi.astype(F32)).astype(BF16)
            r_buf[:, cs] = rest
            c_ref[:, cs] = prior + jnp.sum(sp, axis=0, keepdims=True)

    def weights():
        u = u_ref[...]
        for cs in chunks:
            cum = (jnp.dot(u, hi_buf[:, cs], preferred_element_type=F32)
                   + jnp.dot(u, lo_buf[:, cs], preferred_element_type=F32))
            w_buf[:, cs] = jnp.exp(r_buf[:, cs] - cum).astype(BF16)
            lb_ref[:, cs] = lb_ref[:, cs] + cum[0:1]

    def values(n):
        vt = vt_ref[i - jnp.maximum(n, 0)]
        for cs in chunks:
            acc_ref[:, cs] += jnp.dot(vt, w_buf[:, cs], preferred_element_type=F32)

    hand_off(scores(0))
    breaks(True)
    hand_off(scores(jnp.minimum(1, i)))

    flag_ref[0] = 1
    flag_ref[1] = 1

    def body(n, carry):
        @pl.when(flag_ref[0] == 1)
        def _():
            values(n - 2)
            weights()
            breaks(False)
            hand_off(scores(jnp.minimum(n + 1, i)))
            flag_ref[0] = jnp.max(jnp.where(lb_ref[...] <= SB_DEAD_LOG, 1, 0))
            flag_ref[1] = n + 1
        return carry

    lax.fori_loop(1, i + 1, body, 0)
    n_done = flag_ref[1]
    values(n_done - 2)

    @pl.when(flag_ref[0] == 1)
    def _():
        weights()
        values(n_done - 1)

    def head_norm(o):
        return o * lax.rsqrt(jnp.mean(o * o, axis=0, keepdims=True) + EPS)

    o = jnp.concatenate([head_norm(acc_ref[0:64, 0:tq]), head_norm(acc_ref[64:128, tq:])], axis=0)
    o_ref[...] = (o.T * g_ref[...]).astype(o_ref.dtype)


def _stick_breaking(bq, bk, bvt, norm_g, batch, seq):
    tq = SB_TILE
    nq = seq // tq
    u = (lax.broadcasted_iota(jnp.int32, (tq, tq), 1) > lax.broadcasted_iota(jnp.int32, (tq, tq), 0)).astype(BF16)
    kern = functools.partial(_sb_kernel, tq=tq, cw=ATT_CHUNK)
    qspec = pl.BlockSpec((tq, 128), lambda b, h, i: (b * nq + i, h))
    kspec = pl.BlockSpec((seq, 128), lambda b, h, i: (b, h))
    vspec = pl.BlockSpec((nq, 128, tq), lambda b, h, i: (b, h, 0))
    return pl.pallas_call(
        kern,
        grid=(batch, N_HEADS_B // 2, nq),
        in_specs=[pl.BlockSpec((1, 128), lambda b, h, i: (0, h)),
                  pl.BlockSpec(u.shape, lambda b, h, i: (0, 0)),
                  qspec, kspec, vspec],
        out_specs=qspec,
        out_shape=jax.ShapeDtypeStruct(bq.shape, BF16),
        scratch_shapes=[pltpu.VMEM((2 * tq, 128), BF16), pltpu.VMEM((1, 2 * tq), F32),
                        pltpu.VMEM((128, 2 * tq), F32),
                        pltpu.VMEM((tq, 2 * tq), F32), pltpu.VMEM((tq, 2 * tq), BF16),
                        pltpu.VMEM((tq, 2 * tq), BF16), pltpu.VMEM((tq, 2 * tq), F32),
                        pltpu.VMEM((tq, 2 * tq), BF16), pltpu.VMEM((1, 2 * tq), F32),
                        pltpu.SMEM((2,), jnp.int32)],
        compiler_params=pltpu.CompilerParams(dimension_semantics=("parallel", "parallel", "arbitrary"),
                                             vmem_limit_bytes=VMEM_LIMIT),
        name="stick_breaking",
    )(norm_g, u, bq, bk, bvt)


def _mix_residual(x_ref, ya_ref, yb_ref, cu_ref, cv_ref, sg_ref, ws_ref, bs_ref, nc_ref, wo_ref, tm):
    vn = _group_rms64(cv_ref[...], sg_ref[...]).astype(BF16)
    tril = lax.broadcasted_iota(jnp.int32, (CHUNK, CHUNK), 0) >= lax.broadcasted_iota(jnp.int32, (CHUNK, CHUNK), 1)
    ws = [jnp.where(tril, ws_ref[g], 0.0).astype(BF16) for g in range(N_GROUPS_C)]
    grp = lax.broadcasted_iota(jnp.int32, (CHUNK, WIDTH_C), 1) // 64
    bias = bs_ref[...]
    chunks = []
    for n in range(tm // CHUNK):
        vc = vn[n * CHUNK:(n + 1) * CHUNK, :]
        s = jnp.zeros((CHUNK, WIDTH_C), F32)
        for g in range(N_GROUPS_C):
            s = jnp.where(grp == g, jnp.dot(ws[g], vc, preferred_element_type=F32), s)
        chunks.append(s + bias)
    yc = cu_ref[...] * jnp.concatenate(chunks, axis=0)
    yc = _group_rms64(yc, nc_ref[...]).astype(BF16)
    return (x_ref[...]
            + jnp.dot(ya_ref[...], wo_ref[0:512, :], preferred_element_type=F32)
            + jnp.dot(yb_ref[...], wo_ref[512:768, :], preferred_element_type=F32)
            + jnp.dot(yc, wo_ref[768:1024, :], preferred_element_type=F32))


def _memkv_kernel(m_ref, g_ref, wk_ref, wv_ref, k_ref, v_ref):
    h = _rms(m_ref[0], g_ref[0]).astype(BF16)
    k_ref[0, 0] = jnp.dot(h, wk_ref[0].astype(BF16), preferred_element_type=F32).astype(BF16)
    v_ref[0, 0] = jnp.dot(h, wv_ref[0].astype(BF16), preferred_element_type=F32).astype(BF16)


def _memkv(mem, norm_mem, w_ck, w_cv):
    depth = w_ck.shape[0]
    batch = mem.shape[0]
    out = jax.ShapeDtypeStruct((depth, batch, MEM_LEN, CROSS_WIDTH), BF16)
    wspec = pl.BlockSpec((1, D_MODEL, CROSS_WIDTH), lambda l, b: (l, 0, 0))
    ospec = pl.BlockSpec((1, 1, MEM_LEN, CROSS_WIDTH), lambda l, b: (l, b, 0, 0))
    return pl.pallas_call(
        _memkv_kernel,
        grid=(depth, batch),
        in_specs=[pl.BlockSpec((1, MEM_LEN, D_MODEL), lambda l, b: (b, 0, 0)),
                  pl.BlockSpec((1, 1, D_MODEL), lambda l, b: (l, 0, 0)), wspec, wspec],
        out_specs=[ospec, ospec],
        out_shape=[out, out],
        compiler_params=pltpu.CompilerParams(dimension_semantics=("parallel", "parallel"),
                                             vmem_limit_bytes=VMEM_LIMIT),
        name="mem_kv",
    )(mem, norm_mem.reshape(depth, 1, D_MODEL), w_ck, w_cv)


def _cross_kernel(x_ref, ya_ref, yb_ref, cu_ref, cv_ref, sg_ref, ws_ref, bs_ref, ncout_ref, wmix_ref,
                  nc_ref, wq_ref, km_ref, vm_ref, wo_ref, nf_ref, wrt_ref, brt_ref, u_ref,
                  o_ref, cls_ref, rank_ref, cnt_ref, carry_ref, *, tm):
    @pl.when(pl.program_id(0) == 0)
    def _():
        carry_ref[...] = jnp.zeros_like(carry_ref)

    x = _mix_residual(x_ref, ya_ref, yb_ref, cu_ref, cv_ref, sg_ref, ws_ref, bs_ref, ncout_ref, wmix_ref, tm)
    h = _rms(x, nc_ref[...]).astype(BF16)
    q = (jnp.dot(h, wq_ref[...], preferred_element_type=F32) * 0.125).astype(BF16)
    km = km_ref[0]
    vm = vm_ref[0]
    qgrp = lax.broadcasted_iota(jnp.int32, (tm, CROSS_WIDTH), 1) // 64
    vgrp = lax.broadcasted_iota(jnp.int32, (MEM_LEN, CROSS_WIDTH), 1) // 64
    o = jnp.zeros((tm, CROSS_WIDTH), F32)
    for hh in range(4):
        s = _nt_dot(jnp.where(qgrp == hh, q, jnp.zeros_like(q)), km)
        p = jnp.exp(s - jnp.max(s, axis=-1, keepdims=True))
        p = p / jnp.sum(p, axis=-1, keepdims=True)
        o = o + jnp.dot(p.astype(BF16), jnp.where(vgrp == hh, vm, jnp.zeros_like(vm)),
                        preferred_element_type=F32)
    x2 = x + jnp.dot(o.astype(BF16), wo_ref[...], preferred_element_type=F32)
    o_ref[...] = x2

    hf = _rms(x2, nf_ref[...])
    w_hi, w_lo = _split_bf16(wrt_ref[...])
    h_hi, h_lo = _split_bf16(hf)
    lt = (_nt_dot(w_hi, h_hi) + _nt_dot(w_hi, h_lo) + _nt_dot(w_lo, h_hi)) + brt_ref[...]
    gl = [lt[r:r + 1, :] for r in range(4)]
    el = [lt[4 + r:5 + r, :] for r in range(16)]

    def first_argmax(vals):
        mx = functools.reduce(jnp.maximum, vals)
        idx = jnp.full(vals[0].shape, len(vals) - 1, jnp.int32)
        for r in range(len(vals) - 2, -1, -1):
            idx = jnp.where(vals[r] == mx, r, idx)
        return idx

    g = first_argmax(gl)
    es = [jnp.where(g == 0, el[r], jnp.where(g == 1, el[4 + r], jnp.where(g == 2, el[8 + r], el[12 + r])))
          for r in range(4)]
    i1 = first_argmax(es)
    i2 = first_argmax([jnp.where(i1 == r, -jnp.inf, es[r]) for r in range(4)])
    a = jnp.minimum(i1, i2)
    b = jnp.maximum(i1, i2)
    cls = g * 6 + jnp.where(a == 0, b - 1, jnp.where(a == 1, b + 1, 5))

    onehot = lax.broadcasted_iota(jnp.int32, (CLASS_ROWS, tm), 0) == cls
    oh = jnp.where(onehot, 1.0, 0.0)
    before = jnp.dot(oh.astype(BF16), u_ref[...], preferred_element_type=F32) + carry_ref[...]
    rank = jnp.sum(jnp.where(onehot, before, 0.0), axis=0, keepdims=True)
    carry_ref[...] += jnp.sum(oh, axis=1, keepdims=True)
    cls_ref[0] = cls
    rank_ref[0] = rank.astype(jnp.int32)
    cnt_ref[...] = jnp.broadcast_to(carry_ref[...], cnt_ref.shape).astype(jnp.int32)


def _mix_cross_router(x2, ya, yb, cu, cv, sgu_g, w_s, b_full, ncout_g, wmix_bf,
                      nc_g, wq_bf, kmem, vmem, wo_bf, nf_g, wrt_t, brt_col, seq):
    t = x2.shape[0]
    tm = ROW_TILE
    nt = t // tm
    per_b = seq // tm
    u = (lax.broadcasted_iota(jnp.int32, (tm, tm), 0) < lax.broadcasted_iota(jnp.int32, (tm, tm), 1)).astype(BF16)
    row = pl.BlockSpec((tm, D_MODEL), lambda i: (i, 0))
    roww = lambda w: pl.BlockSpec((tm, w), lambda i: (i, 0))
    full = lambda a: pl.BlockSpec(a.shape, lambda i: (0,) * a.ndim)
    mspec = pl.BlockSpec((1, MEM_LEN, CROSS_WIDTH), lambda i: (i // per_b, 0, 0))
    tokspec = pl.BlockSpec((1, 1, tm), lambda i: (i, 0, 0))
    return pl.pallas_call(
        functools.partial(_cross_kernel, tm=tm),
        grid=(nt,),
        in_specs=[row, roww(512), roww(256), roww(256), roww(256),
                  full(sgu_g), full(w_s), full(b_full), full(ncout_g), full(wmix_bf),
                  full(nc_g), full(wq_bf), mspec, mspec, full(wo_bf), full(nf_g), full(wrt_t),
                  full(brt_col), full(u)],
        out_specs=[row, tokspec, tokspec, pl.BlockSpec((CLASS_ROWS, 128), lambda i: (0, 0))],
        out_shape=[jax.ShapeDtypeStruct(x2.shape, F32), jax.ShapeDtypeStruct((nt, 1, tm), jnp.int32),
                   jax.ShapeDtypeStruct((nt, 1, tm), jnp.int32),
                   jax.ShapeDtypeStruct((CLASS_ROWS, 128), jnp.int32)],
        scratch_shapes=[pltpu.VMEM((CLASS_ROWS, 1), F32)],
        compiler_params=pltpu.CompilerParams(dimension_semantics=("arbitrary",), vmem_limit_bytes=VMEM_LIMIT),
        name="mix_cross_router",
    )(x2, ya, yb, cu, cv, sgu_g, w_s, b_full, ncout_g, wmix_bf,
      nc_g, wq_bf, kmem, vmem, wo_bf, nf_g, wrt_t, brt_col, u)


def _moe_kernel(e1_ref, e2_ref, nv_ref, src_ref,
                x_hbm, nf_ref, wrt_ref, brt_ref, wg1_ref, wu1_ref, wd1_ref, wg2_ref, wu2_ref, wd2_ref,
                nfin_ref, out_hbm, xbuf, obuf, wgc, wuc, wdc, gsem, ssem, *, tile, n_tiles, final):
    k = pl.program_id(0)
    slot = k % 2
    prev = jnp.maximum(k - 1, 0)

    def gather_row(kk, sl, r):
        tok = src_ref[kk * tile + r]
        return pltpu.make_async_copy(x_hbm.at[pl.ds(tok, 1), :], xbuf.at[sl, pl.ds(r, 1), :], gsem.at[sl])

    def scatter_row(kk, sl, r):
        tok = src_ref[kk * tile + r]
        return pltpu.make_async_copy(obuf.at[sl, pl.ds(r, 1), :], out_hbm.at[pl.ds(tok, 1), :], ssem.at[sl])

    def start_gather(kk, sl):
        def body8(g, c):
            for r in range(8):
                gather_row(kk, sl, g * 8 + r).start(priority=r % 2)
            return c
        lax.fori_loop(0, tile // 8, body8, 0)

    def wait_gather(sl):
        pltpu.make_async_copy(x_hbm.at[pl.ds(0, tile), :], xbuf.at[sl], gsem.at[sl]).wait()

    def wait_scatter(n, sl):
        n8 = pl.multiple_of((n // 8) * 8, 8)

        @pl.when(n8 > 0)
        def _():
            pltpu.make_async_copy(obuf.at[sl, pl.ds(0, n8), :], out_hbm.at[pl.ds(0, n8), :], ssem.at[sl]).wait()

        def body(r, c):
            pltpu.make_async_copy(obuf.at[sl, pl.ds(0, 1), :], out_hbm.at[pl.ds(0, 1), :], ssem.at[sl]).wait()
            return c
        lax.fori_loop(n8, n, body, 0)

    @pl.when(jnp.logical_and(k == 0, nv_ref[0] > 0))
    def _():
        start_gather(0, 0)

    @pl.when(k >= 2)
    def _():
        wait_scatter(nv_ref[jnp.maximum(k - 2, 0)], slot)

    nxt = jnp.minimum(k + 1, n_tiles - 1)

    @pl.when(jnp.logical_and(k + 1 < n_tiles, nv_ref[nxt] > 0))
    def _():
        start_gather(k + 1, 1 - slot)

    @pl.when(nv_ref[k] > 0)
    def _():
        @pl.when(jnp.logical_or(k == 0, e1_ref[k] != e1_ref[prev]))
        def _():
            wgc[0] = wg1_ref[0, 0].astype(BF16)
            wuc[0] = wu1_ref[0, 0].astype(BF16)
            wdc[0] = wd1_ref[0, 0].astype(BF16)

        @pl.when(jnp.logical_or(k == 0, e2_ref[k] != e2_ref[prev]))
        def _():
            wgc[1] = wg2_ref[0, 0].astype(BF16)
            wuc[1] = wu2_ref[0, 0].astype(BF16)
            wdc[1] = wd2_ref[0, 0].astype(BF16)

        wait_gather(slot)
        x = xbuf[slot]
        hf = _rms(x, nf_ref[...])
        h = hf.astype(BF16)
        w_hi, w_lo = _split_bf16(wrt_ref[...])
        h_lo = (hf - h.astype(F32)).astype(BF16)
        lg = (jnp.dot(h, w_hi, preferred_element_type=F32) + jnp.dot(h, w_lo, preferred_element_type=F32)
              + jnp.dot(h_lo, w_hi, preferred_element_type=F32)) + brt_ref[...]
        lane = lax.broadcasted_iota(jnp.int32, lg.shape, 1)
        e1 = e1_ref[k]
        e2 = e2_ref[k]
        glm = jnp.where(lane < N_EXPERT_GROUPS, lg, -jnp.inf)
        eg = jnp.exp(glm - jnp.max(glm, axis=-1, keepdims=True))
        pick = lambda a, idx: jnp.sum(jnp.where(lane == idx, a, 0.0), axis=-1, keepdims=True)
        p_group = pick(eg, e1 // EXPERTS_PER_GROUP) / jnp.sum(eg, axis=-1, keepdims=True)
        la = pick(lg, N_EXPERT_GROUPS + e1)
        lb = pick(lg, N_EXPERT_GROUPS + e2)
        mx = jnp.maximum(la, lb)
        ea = jnp.exp(la - mx)
        eb = jnp.exp(lb - mx)
        wa = ea / (ea + eb) * p_group
        wb = eb / (ea + eb) * p_group

        def expert(s):
            gt = jnp.dot(h, wgc[s], preferred_element_type=F32)
            up = jnp.dot(h, wuc[s], preferred_element_type=F32)
            act = (jax.nn.silu(gt) * up).astype(BF16)
            return jnp.dot(act, wdc[s], preferred_element_type=F32)

        y = x + (wa * expert(0) + wb * expert(1))
        if final:
            y = _rms(y, nfin_ref[...])
        obuf[slot] = y

        n_rows = nv_ref[k]

        def body8(g, c):
            for r in range(8):
                scatter_row(k, slot, g * 8 + r).start(priority=r % 2)
            return c
        lax.fori_loop(0, n_rows // 8, body8, 0)

        def body(r, c):
            scatter_row(k, slot, r).start()
            return c
        lax.fori_loop((n_rows // 8) * 8, n_rows, body, 0)

    @pl.when(k == n_tiles - 1)
    def _():
        wait_scatter(nv_ref[k], slot)
        wait_scatter(nv_ref[prev], 1 - slot)


def _moe(x2, e1, e2, nv, src, nf_g, wrt, brt_row, w_gate, w_up, w_down, layer, nfin_g, final):
    t = x2.shape[0]
    tile = MOE_TILE
    n_tiles = e1.shape[0]
    full = lambda a: pl.BlockSpec(a.shape, lambda k, *_: (0,) * a.ndim)
    wspec1 = lambda a: pl.BlockSpec((1, 1) + a.shape[2:], lambda k, e1r, e2r, nvr, srcr: (layer, e1r[k], 0, 0))
    wspec2 = lambda a: pl.BlockSpec((1, 1) + a.shape[2:], lambda k, e1r, e2r, nvr, srcr: (layer, e2r[k], 0, 0))
    anyspec = pl.BlockSpec(memory_space=pl.ANY)
    grid_spec = pltpu.PrefetchScalarGridSpec(
        num_scalar_prefetch=4,
        grid=(n_tiles,),
        in_specs=[anyspec, full(nf_g), full(wrt), full(brt_row),
                  wspec1(w_gate), wspec1(w_up), wspec1(w_down),
                  wspec2(w_gate), wspec2(w_up), wspec2(w_down), full(nfin_g)],
        out_specs=anyspec,
        scratch_shapes=[pltpu.VMEM((2, tile, D_MODEL), F32), pltpu.VMEM((2, tile, D_MODEL), F32),
                        pltpu.VMEM((2, D_MODEL, D_EXPERT), BF16), pltpu.VMEM((2, D_MODEL, D_EXPERT), BF16),
                        pltpu.VMEM((2, D_EXPERT, D_MODEL), BF16),
                        pltpu.SemaphoreType.DMA((2,)), pltpu.SemaphoreType.DMA((2,))],
    )
    return pl.pallas_call(
        functools.partial(_moe_kernel, tile=tile, n_tiles=n_tiles, final=final),
        grid_spec=grid_spec,
        out_shape=jax.ShapeDtypeStruct((t, D_MODEL), F32),
        compiler_params=pltpu.CompilerParams(dimension_semantics=("arbitrary",), vmem_limit_bytes=VMEM_LIMIT),
        name="moe_experts",
    )(e1, e2, nv, src, x2, nf_g, wrt, brt_row, w_gate, w_up, w_down, w_gate, w_up, w_down, nfin_g)


def _route_meta(cls, rank, counts, t, tile, n_tiles):
    tiles_per = (counts + tile - 1) // tile
    tile_end = jnp.cumsum(tiles_per)
    tile_start = tile_end - tiles_per
    pos = tile_start[cls] * tile + rank
    src = jnp.zeros((n_tiles * tile,), jnp.int32).at[pos].set(jnp.arange(t, dtype=jnp.int32), unique_indices=True)
    kk = jnp.arange(n_tiles, dtype=jnp.int32)
    tcls = jnp.sum((kk[:, None] >= tile_end[None, :]).astype(jnp.int32), axis=1)
    valid = tcls < N_CLASSES
    last_cls = tcls[jnp.maximum(tile_end[-1] - 1, 0)]
    tc = jnp.minimum(jnp.where(valid, tcls, last_cls), N_CLASSES - 1)
    nv = jnp.where(valid, jnp.clip(counts[tc] - (kk - tile_start[tc]) * tile, 0, tile), 0)
    grp = tc // 6
    e1 = grp * EXPERTS_PER_GROUP + jnp.asarray(PAIR_LO, jnp.int32)[tc % 6]
    e2 = grp * EXPERTS_PER_GROUP + jnp.asarray(PAIR_HI, jnp.int32)[tc % 6]
    return e1.astype(jnp.int32), e2.astype(jnp.int32), nv.astype(jnp.int32), src


def _rotary_lane_tables(positions):
    lane = jnp.arange(128, dtype=jnp.int32) % DIFF_QK_DIM
    half = ROPE_DIM // 2
    inv_freq = ROPE_THETA ** (-(2 * (lane % half)).astype(F32) / ROPE_DIM)
    inv_freq = jnp.where(lane < ROPE_DIM, inv_freq, 0.0)
    sign = jnp.where(lane < half, -1.0, 1.0).astype(F32)
    ang = positions.astype(F32).reshape(-1, 1) * inv_freq[None, :]
    return jnp.cos(ang), jnp.sin(ang) * sign[None, :]


def kernel(x, mem, positions, norm_mix, w_in, lam_q1, lam_k1, lam_q2, lam_k2, subln_a, norm_b_out, sgu_norm, w_spatial, b_spatial, norm_c_out, w_out, norm_cross, norm_mem, w_cq, w_ck, w_cv, w_co, norm_ffn, w_group, b_group, w_router, b_router, w_gate, w_up, w_down, norm_final):
    batch, seq, d = x.shape
    depth = w_in.shape[0]
    t = batch * seq
    n_tiles = t // MOE_TILE + N_CLASSES
    x2 = x.reshape(t, d)
    rc, rs = _rotary_lane_tables(positions)
    kmem, vmem = _memkv(mem, norm_mem, w_ck, w_cv)
    row = lambda a: a.reshape(1, -1)
    for l in range(depth):
        lam_init = 0.8 - 0.6 * math.exp(-0.3 * l)
        w_vt = jnp.concatenate([w_in[l][:, 1024:1536], w_in[l][:, 2048:2304]], axis=1).T.astype(BF16)
        aq, ak, avt, bq, bk, bvt, cu, cv = _inproj(x2, row(norm_mix[l]), w_in[l].astype(BF16), w_vt, rc, rs)
        lam_p = jnp.stack([lam_q1[l], lam_k1[l], lam_q2[l], lam_k2[l]])
        ya = _diff_attention(aq, ak, avt, lam_p, row(subln_a[l]), lam_init, batch, seq)
        yb = _stick_breaking(bq, bk, bvt, row(norm_b_out[l]), batch, seq)
        b_full = jnp.repeat(b_spatial[l].T, 64, axis=1)
        wrt = jnp.concatenate([w_group[l], w_router[l],
                               jnp.zeros((d, ROUTER_LANES - N_EXPERT_GROUPS - N_EXPERTS), F32)], axis=1)
        brt = jnp.concatenate([b_group[l], b_router[l],
                               jnp.zeros((ROUTER_LANES - N_EXPERT_GROUPS - N_EXPERTS,), F32)])
        x2, cls, rank, cnt = _mix_cross_router(
            x2, ya, yb, cu, cv, row(sgu_norm[l]), w_spatial[l], b_full, row(norm_c_out[l]), w_out[l].astype(BF16),
            row(norm_cross[l]), w_cq[l].astype(BF16), kmem[l], vmem[l], w_co[l].astype(BF16), row(norm_ffn[l]),
            wrt.T, brt.reshape(-1, 1), seq)
        e1, e2, nv, src = _route_meta(cls.reshape(t), rank.reshape(t), cnt[:N_CLASSES, 0], t, MOE_TILE, n_tiles)
        x2 = _moe(x2, e1, e2, nv, src, row(norm_ffn[l]), wrt, brt.reshape(1, -1), w_gate, w_up, w_down, l,
                  row(norm_final), final=(l == depth - 1))
    return x2.reshape(batch, seq, d)
```

```python
import functools
import math

import jax
import jax.numpy as jnp
from jax import lax
from jax.experimental import pallas as pl
from jax.experimental.pallas import tpu as pltpu

F32 = jnp.float32
BF16 = jnp.bfloat16
EPS = 1e-6

D_MODEL = 1024
N_HEADS_A = 4
DIFF_QK_DIM = 64
N_HEADS_B = 4
SB_DIM = 64
N_GROUPS_C = 4
CHUNK = 128
WIDTH_A = 512
WIDTH_B = 256
WIDTH_C = 256
D_IN = 2816
ROPE_THETA = 500000.0
ROPE_DIM = 16
MEM_LEN = 256
CROSS_WIDTH = 256
N_EXPERT_GROUPS = 4
EXPERTS_PER_GROUP = 4
N_EXPERTS = 16
D_EXPERT = 512
N_CLASSES = 24
CLASS_ROWS = 32
ROUTER_LANES = 128
PAIR_LO = (0, 0, 0, 1, 1, 2)
PAIR_HI = (1, 2, 3, 2, 3, 3)

ROW_TILE = 512
DIFF_TILE = 512
SB_TILE = 256
ATT_CHUNK = 256
SOFTMAX_SLAB = 128
LOG2E = 1.4426950408889634
SB_DEAD_LOG = 110.0
MOE_TILE = 256
VMEM_LIMIT = 56 * 1024 * 1024


def _rms(x, g):
    return x * lax.rsqrt(jnp.mean(x * x, axis=-1, keepdims=True) + EPS) * g


def _group_rms64(x, g):
    m, w = x.shape
    grp = lax.broadcasted_iota(jnp.int32, (m, w), 1) // 64
    sq = x * x
    scale = jnp.zeros_like(x)
    for gi in range(w // 64):
        sel = grp == gi
        ss = jnp.sum(jnp.where(sel, sq, 0.0), axis=-1, keepdims=True)
        scale = jnp.where(sel, lax.rsqrt(ss * (1.0 / 64.0) + EPS), scale)
    return x * scale * g


def _split_bf16(x):
    hi = x.astype(BF16)
    return hi, (x - hi.astype(F32)).astype(BF16)


def _nt_dot(a, b, **kw):
    return lax.dot_general(a, b, (((1,), (1,)), ((), ())), preferred_element_type=F32, **kw)


def _inproj_kernel(x_ref, g_ref, w_ref, wvt_ref, c_ref, s_ref,
                   aq_ref, ak_ref, avt_ref, bq_ref, bk_ref, bvt_ref, cu_ref, cv_ref, *, tm):
    h = _rms(x_ref[...], g_ref[...]).astype(BF16)

    def proj(lo, hi):
        return jnp.dot(h, w_ref[:, lo:hi], preferred_element_type=F32)

    c, s = c_ref[...], s_ref[...]
    first_half = lax.broadcasted_iota(jnp.int32, s.shape, 1) % DIFF_QK_DIM < ROPE_DIM // 2
    s1 = jnp.where(first_half, s, 0.0)
    s2 = s - s1

    def rot(t):
        outs = []
        for hh in range(4):
            th = t[:, hh * 128:(hh + 1) * 128]
            outs.append(th * c + pltpu.roll(th, 120, 1) * s1 + pltpu.roll(th, 8, 1) * s2)
        return jnp.concatenate(outs, axis=1)

    aq_ref[...] = (rot(proj(0, 512)) * (0.125 * LOG2E)).astype(BF16)
    ak_ref[...] = rot(proj(512, 1024)).astype(BF16)
    bq_ref[...] = (proj(1536, 1792) * 0.125).astype(BF16)
    bk_ref[...] = proj(1792, 2048).astype(BF16)
    cu_ref[...] = jax.nn.gelu(proj(2304, 2560))
    cv_ref[...] = jax.nn.gelu(proj(2560, 2816))
    vt = _nt_dot(wvt_ref[...], h).astype(BF16)
    for blk in range(tm // DIFF_TILE):
        avt_ref[blk] = vt[0:WIDTH_A, blk * DIFF_TILE:(blk + 1) * DIFF_TILE]
    for blk in range(tm // SB_TILE):
        bvt_ref[blk] = vt[WIDTH_A:, blk * SB_TILE:(blk + 1) * SB_TILE]


def _inproj(x2, g, w_bf, wvt_bf, rc, rs):
    t = x2.shape[0]
    tm = ROW_TILE
    row = lambda w: pl.BlockSpec((tm, w), lambda i: (i, 0))
    full = lambda a: pl.BlockSpec(a.shape, lambda i: (0,) * a.ndim)
    slab = lambda w, tile: pl.BlockSpec((tm // tile, w, tile), lambda i: (i, 0, 0))
    sds = jax.ShapeDtypeStruct
    outs = [sds((t, 512), BF16), sds((t, 512), BF16), sds((t // DIFF_TILE, WIDTH_A, DIFF_TILE), BF16),
            sds((t, 256), BF16), sds((t, 256), BF16), sds((t // SB_TILE, WIDTH_B, SB_TILE), BF16),
            sds((t, 256), F32), sds((t, 256), F32)]
    return pl.pallas_call(
        functools.partial(_inproj_kernel, tm=tm),
        grid=(t // tm,),
        in_specs=[row(D_MODEL), full(g), full(w_bf), full(wvt_bf), row(128), row(128)],
        out_specs=[row(512), row(512), slab(WIDTH_A, DIFF_TILE), row(256), row(256), slab(WIDTH_B, SB_TILE),
                   row(256), row(256)],
        out_shape=outs,
        compiler_params=pltpu.CompilerParams(dimension_semantics=("parallel",), vmem_limit_bytes=VMEM_LIMIT),
        name="inproj",
    )(x2, g, w_bf, wvt_bf, rc, rs)


def _stack_queries(q_ref, qs_ref, tq):
    q = q_ref[...]
    lane = lax.broadcasted_iota(jnp.int32, (tq, 128), 1)
    zero = jnp.zeros_like(q)
    qs_ref[0:tq, :] = jnp.where(lane < 64, q, zero)
    qs_ref[tq:, :] = jnp.where(lane >= 64, q, zero)


def _fold8(x, op):
    n, w = x.shape
    return op(x.reshape(n // 8, 8, w), axis=0)


def _key_le_query(nk, cw, q0, strict):
    key = lax.broadcasted_iota(jnp.int32, (nk, cw), 0)
    qry = lax.broadcasted_iota(jnp.int32, (nk, cw), 1) + q0
    return key < qry if strict else key <= qry


def _diff_kernel(lam_ref, g_ref, q_ref, k_ref, vt_ref, o_ref, qs_ref, m_ref, l_ref, acc_ref,
                 s_buf, p_buf, a_buf, *, tq, cw, lam_init):
    i = pl.program_id(2)
    chunks = [slice(c * cw, (c + 1) * cw) for c in range(2 * tq // cw)]
    _stack_queries(q_ref, qs_ref, tq)
    m_ref[...] = jnp.full_like(m_ref, -jnp.inf)
    l_ref[...] = jnp.zeros_like(l_ref)
    acc_ref[...] = jnp.zeros_like(acc_ref)
    p_buf[...] = jnp.zeros_like(p_buf)
    a_buf[...] = jnp.ones_like(a_buf)

    def scores(j):
        k = k_ref[pl.ds(pl.multiple_of(j * tq, tq), tq), :]
        return [_nt_dot(k, qs_ref[cs, :]) for cs in chunks]

    def hand_off(s_new):
        for c, cs in enumerate(chunks):
            s_buf[:, cs] = s_new[c]

    def softmax(masked):
        for c, cs in enumerate(chunks):
            q0 = (c * cw) % tq
            nk = q0 + cw if masked else tq
            slabs = [slice(r, r + SOFTMAX_SLAB) for r in range(0, nk, SOFTMAX_SLAB)]

            def load(rs):
                sc = s_buf[rs, cs]
                if masked and rs.stop > q0:
                    keep = _key_le_query(SOFTMAX_SLAB, cw, q0 - rs.start, False)
                    sc = jnp.where(keep, sc, -jnp.inf)
                return sc

            m_old = m_ref[:, cs]
            m8 = functools.reduce(jnp.maximum, [_fold8(load(rs), jnp.max) for rs in slabs])
            m_new = jnp.maximum(m_old, jnp.max(m8, axis=0, keepdims=True))
            alpha = jnp.exp2(m_old - m_new)
            l8 = []
            for rs in slabs:
                p = jnp.exp2(load(rs) - m_new)
                l8.append(_fold8(p, jnp.sum))
                p_buf[rs, cs] = p.astype(BF16)
            l_ref[:, cs] = alpha * l_ref[:, cs] + jnp.sum(functools.reduce(jnp.add, l8), axis=0, keepdims=True)
            m_ref[:, cs] = m_new
            if nk < tq:
                p_buf[nk:, cs] = jnp.zeros((tq - nk, cw), BF16)
            a_buf[:, cs] = alpha

    def values_matmul(j):
        vt = vt_ref[jnp.maximum(j, 0)]
        return [(a_buf[:, cs], jnp.dot(vt, p_buf[:, cs], preferred_element_type=F32)) for cs in chunks]

    def accumulate(scaled):
        for cs, (alpha, pv) in zip(chunks, scaled):
            acc_ref[:, cs] = alpha * acc_ref[:, cs] + pv

    def values(j):
        accumulate(values_matmul(j))

    hand_off(scores(0))

    def body(t, carry):
        s_new = scores(t + 1)
        scaled = values_matmul(t - 1)
        softmax(False)
        accumulate(scaled)
        hand_off(s_new)
        return carry

    lax.fori_loop(0, i, body, 0)
    values(i - 1)
    softmax(True)
    values(i)

    lp = lam_ref[...]
    lam = (jnp.exp(jnp.sum(lp[0:1] * lp[1:2], axis=-1, keepdims=True))
           - jnp.exp(jnp.sum(lp[2:3] * lp[3:4], axis=-1, keepdims=True)) + lam_init)
    o = acc_ref[...] / l_ref[...]
    o = o[:, 0:tq] - lam * o[:, tq:]
    o = o * lax.rsqrt(jnp.mean(o * o, axis=0, keepdims=True) + EPS)
    o_ref[...] = (o.T * g_ref[...] * (1.0 - lam_init)).astype(o_ref.dtype)


def _diff_attention(aq, ak, avt, lam_p, subln_g, lam_init, batch, seq):
    tq = DIFF_TILE
    nq = seq // tq
    kern = functools.partial(_diff_kernel, tq=tq, cw=ATT_CHUNK, lam_init=lam_init)
    qspec = pl.BlockSpec((tq, 128), lambda b, h, i: (b * nq + i, h))
    kspec = pl.BlockSpec((seq, 128), lambda b, h, i: (b, h))
    vspec = pl.BlockSpec((nq, 128, tq), lambda b, h, i: (b, h, 0))
    return pl.pallas_call(
        kern,
        grid=(batch, N_HEADS_A, nq),
        in_specs=[pl.BlockSpec(lam_p.shape, lambda b, h, i: (0, 0)),
                  pl.BlockSpec(subln_g.shape, lambda b, h, i: (0, 0)),
                  qspec, kspec, vspec],
        out_specs=qspec,
        out_shape=jax.ShapeDtypeStruct(aq.shape, BF16),
        scratch_shapes=[pltpu.VMEM((2 * tq, 128), BF16), pltpu.VMEM((1, 2 * tq), F32),
                        pltpu.VMEM((1, 2 * tq), F32), pltpu.VMEM((128, 2 * tq), F32),
                        pltpu.VMEM((tq, 2 * tq), F32), pltpu.VMEM((tq, 2 * tq), BF16),
                        pltpu.VMEM((1, 2 * tq), F32)],
        compiler_params=pltpu.CompilerParams(dimension_semantics=("parallel", "parallel", "arbitrary"),
                                             vmem_limit_bytes=VMEM_LIMIT),
        name="diff_attn",
    )(lam_p, subln_g, aq, ak, avt)


def _sb_kernel(g_ref, u_ref, q_ref, k_ref, vt_ref, o_ref, qs_ref, c_ref, acc_ref,
               z_buf, hi_buf, lo_buf, r_buf, w_buf, lb_ref, flag_ref, *, tq, cw):
    i = pl.program_id(2)
    chunks = [slice(c * cw, (c + 1) * cw) for c in range(2 * tq // cw)]
    _stack_queries(q_ref, qs_ref, tq)
    c_ref[...] = jnp.zeros_like(c_ref)
    acc_ref[...] = jnp.zeros_like(acc_ref)
    w_buf[...] = jnp.zeros_like(w_buf)
    lb_ref[...] = jnp.zeros_like(lb_ref)

    def scores(n):
        k = k_ref[pl.ds(pl.multiple_of((i - n) * tq, tq), tq), :]
        return [_nt_dot(k, qs_ref[cs, :]) for cs in chunks]

    def hand_off(z_new):
        for c, cs in enumerate(chunks):
            z_buf[:, cs] = z_new[c]

    def breaks(masked):
        for c, cs in enumerate(chunks):
            z = z_buf[:, cs]
            sp = jnp.maximum(z, 0.0) + jnp.log(1.0 + jnp.exp2(jnp.abs(z) * -LOG2E))
            prior = c_ref[:, cs]
            rest = z - sp - prior
            if masked:
                strict = _key_le_query(tq, cw, (c * cw) % tq, True)
                sp = jnp.where(strict, sp, 0.0)
                rest = jnp.where(strict, rest, -jnp.inf)
            hi = sp.astype(BF16)
            hi_buf[:, cs] = hi
            lo_buf[:, cs] = (sp - hi.astype(F32)).astype(BF16)
            r_buf[:, cs] = rest
            c_ref[:, cs] = prior + jnp.sum(sp, axis=0, keepdims=True)

    def weights():
        u = u_ref[...]
        for cs in chunks:
            cum = (jnp.dot(u, hi_buf[:, cs], preferred_element_type=F32)
                   + jnp.dot(u, lo_buf[:, cs], preferred_element_type=F32))
            w_buf[:, cs] = jnp.exp(r_buf[:, cs] - cum).astype(BF16)
            lb_ref[:, cs] = lb_ref[:, cs] + cum[0:1]

    def values(n):
        vt = vt_ref[i - jnp.maximum(n, 0)]
        for cs in chunks:
            acc_ref[:, cs] += jnp.dot(vt, w_buf[:, cs], preferred_element_type=F32)

    hand_off(scores(0))
    breaks(True)
    hand_off(scores(jnp.minimum(1, i)))

    flag_ref[0] = 1
    flag_ref[1] = 1

    def body(n, carry):
        @pl.when(flag_ref[0] == 1)
        def _():
            values(n - 2)
            weights()
            breaks(False)
            hand_off(scores(jnp.minimum(n + 1, i)))
            flag_ref[0] = jnp.max(jnp.where(lb_ref[...] <= SB_DEAD_LOG, 1, 0))
            flag_ref[1] = n + 1
        return carry

    lax.fori_loop(1, i + 1, body, 0)
    n_done = flag_ref[1]
    values(n_done - 2)

    @pl.when(flag_ref[0] == 1)
    def _():
        weights()
        values(n_done - 1)

    def head_norm(o):
        return o * lax.rsqrt(jnp.mean(o * o, axis=0, keepdims=True) + EPS)

    o = jnp.concatenate([head_norm(acc_ref[0:64, 0:tq]), head_norm(acc_ref[64:128, tq:])], axis=0)
    o_ref[...] = (o.T * g_ref[...]).astype(o_ref.dtype)


def _stick_breaking(bq, bk, bvt, norm_g, batch, seq):
    tq = SB_TILE
    nq = seq // tq
    u = (lax.broadcasted_iota(jnp.int32, (tq, tq), 1) > lax.broadcasted_iota(jnp.int32, (tq, tq), 0)).astype(BF16)
    kern = functools.partial(_sb_kernel, tq=tq, cw=ATT_CHUNK)
    qspec = pl.BlockSpec((tq, 128), lambda b, h, i: (b * nq + i, h))
    kspec = pl.BlockSpec((seq, 128), lambda b, h, i: (b, h))
    vspec = pl.BlockSpec((nq, 128, tq), lambda b, h, i: (b, h, 0))
    return pl.pallas_call(
        kern,
        grid=(batch, N_HEADS_B // 2, nq),
        in_specs=[pl.BlockSpec((1, 128), lambda b, h, i: (0, h)),
                  pl.BlockSpec(u.shape, lambda b, h, i: (0, 0)),
                  qspec, kspec, vspec],
        out_specs=qspec,
        out_shape=jax.ShapeDtypeStruct(bq.shape, BF16),
        scratch_shapes=[pltpu.VMEM((2 * tq, 128), BF16), pltpu.VMEM((1, 2 * tq), F32),
                        pltpu.VMEM((128, 2 * tq), F32),
                        pltpu.VMEM((tq, 2 * tq), F32), pltpu.VMEM((tq, 2 * tq), BF16),
                        pltpu.VMEM((tq, 2 * tq), BF16), pltpu.VMEM((tq, 2 * tq), F32),
                        pltpu.VMEM((tq, 2 * tq), BF16), pltpu.VMEM((1, 2 * tq), F32),
                        pltpu.SMEM((2,), jnp.int32)],
        compiler_params=pltpu.CompilerParams(dimension_semantics=("parallel", "parallel", "arbitrary"),
                                             vmem_limit_bytes=VMEM_LIMIT),
        name="stick_breaking",
    )(norm_g, u, bq, bk, bvt)


def _mix_residual(x_ref, ya_ref, yb_ref, cu_ref, cv_ref, sg_ref, ws_ref, bs_ref, nc_ref, wo_ref, tm):
    vn = _group_rms64(cv_ref[...], sg_ref[...]).astype(BF16)
    tril = lax.broadcasted_iota(jnp.int32, (CHUNK, CHUNK), 0) >= lax.broadcasted_iota(jnp.int32, (CHUNK, CHUNK), 1)
    ws = [jnp.where(tril, ws_ref[g], 0.0).astype(BF16) for g in range(N_GROUPS_C)]
    grp = lax.broadcasted_iota(jnp.int32, (CHUNK, WIDTH_C), 1) // 64
    bias = bs_ref[...]
    chunks = []
    for n in range(tm // CHUNK):
        vc = vn[n * CHUNK:(n + 1) * CHUNK, :]
        s = jnp.zeros((CHUNK, WIDTH_C), F32)
        for g in range(N_GROUPS_C):
            s = jnp.where(grp == g, jnp.dot(ws[g], vc, preferred_element_type=F32), s)
        chunks.append(s + bias)
    yc = cu_ref[...] * jnp.concatenate(chunks, axis=0)
    yc = _group_rms64(yc, nc_ref[...]).astype(BF16)
    return (x_ref[...]
            + jnp.dot(ya_ref[...], wo_ref[0:512, :], preferred_element_type=F32)
            + jnp.dot(yb_ref[...], wo_ref[512:768, :], preferred_element_type=F32)
            + jnp.dot(yc, wo_ref[768:1024, :], preferred_element_type=F32))


def _memkv_kernel(m_ref, g_ref, wk_ref, wv_ref, k_ref, v_ref):
    h = _rms(m_ref[0], g_ref[0]).astype(BF16)
    k_ref[0, 0] = jnp.dot(h, wk_ref[0].astype(BF16), preferred_element_type=F32).astype(BF16)
    v_ref[0, 0] = jnp.dot(h, wv_ref[0].astype(BF16), preferred_element_type=F32).astype(BF16)


def _memkv(mem, norm_mem, w_ck, w_cv):
    depth = w_ck.shape[0]
    batch = mem.shape[0]
    out = jax.ShapeDtypeStruct((depth, batch, MEM_LEN, CROSS_WIDTH), BF16)
    wspec = pl.BlockSpec((1, D_MODEL, CROSS_WIDTH), lambda l, b: (l, 0, 0))
    ospec = pl.BlockSpec((1, 1, MEM_LEN, CROSS_WIDTH), lambda l, b: (l, b, 0, 0))
    return pl.pallas_call(
        _memkv_kernel,
        grid=(depth, batch),
        in_specs=[pl.BlockSpec((1, MEM_LEN, D_MODEL), lambda l, b: (b, 0, 0)),
                  pl.BlockSpec((1, 1, D_MODEL), lambda l, b: (l, 0, 0)), wspec, wspec],
        out_specs=[ospec, ospec],
        out_shape=[out, out],
        compiler_params=pltpu.CompilerParams(dimension_semantics=("parallel", "parallel"),
                                             vmem_limit_bytes=VMEM_LIMIT),
        name="mem_kv",
    )(mem, norm_mem.reshape(depth, 1, D_MODEL), w_ck, w_cv)


def _cross_kernel(x_ref, ya_ref, yb_ref, cu_ref, cv_ref, sg_ref, ws_ref, bs_ref, ncout_ref, wmix_ref,
                  nc_ref, wq_ref, km_ref, vm_ref, wo_ref, nf_ref, wrt_ref, brt_ref, u_ref,
                  o_ref, cls_ref, rank_ref, cnt_ref, carry_ref, *, tm):
    @pl.when(pl.program_id(0) == 0)
    def _():
        carry_ref[...] = jnp.zeros_like(carry_ref)

    x = _mix_residual(x_ref, ya_ref, yb_ref, cu_ref, cv_ref, sg_ref, ws_ref, bs_ref, ncout_ref, wmix_ref, tm)
    h = _rms(x, nc_ref[...]).astype(BF16)
    q = (jnp.dot(h, wq_ref[...], preferred_element_type=F32) * 0.125).astype(BF16)
    km = km_ref[0]
    vm = vm_ref[0]
    qgrp = lax.broadcasted_iota(jnp.int32, (tm, CROSS_WIDTH), 1) // 64
    vgrp = lax.broadcasted_iota(jnp.int32, (MEM_LEN, CROSS_WIDTH), 1) // 64
    o = jnp.zeros((tm, CROSS_WIDTH), F32)
    for hh in range(4):
        s = _nt_dot(jnp.where(qgrp == hh, q, jnp.zeros_like(q)), km)
        p = jnp.exp(s - jnp.max(s, axis=-1, keepdims=True))
        p = p / jnp.sum(p, axis=-1, keepdims=True)
        o = o + jnp.dot(p.astype(BF16), jnp.where(vgrp == hh, vm, jnp.zeros_like(vm)),
                        preferred_element_type=F32)
    x2 = x + jnp.dot(o.astype(BF16), wo_ref[...], preferred_element_type=F32)
    o_ref[...] = x2

    hf = _rms(x2, nf_ref[...])
    w_hi, w_lo = _split_bf16(wrt_ref[...])
    h_hi, h_lo = _split_bf16(hf)
    lt = (_nt_dot(w_hi, h_hi) + _nt_dot(w_hi, h_lo) + _nt_dot(w_lo, h_hi)) + brt_ref[...]
    gl = [lt[r:r + 1, :] for r in range(4)]
    el = [lt[4 + r:5 + r, :] for r in range(16)]

    def first_argmax(vals):
        mx = functools.reduce(jnp.maximum, vals)
        idx = jnp.full(vals[0].shape, len(vals) - 1, jnp.int32)
        for r in range(len(vals) - 2, -1, -1):
            idx = jnp.where(vals[r] == mx, r, idx)
        return idx

    g = first_argmax(gl)
    es = [jnp.where(g == 0, el[r], jnp.where(g == 1, el[4 + r], jnp.where(g == 2, el[8 + r], el[12 + r])))
          for r in range(4)]
    i1 = first_argmax(es)
    i2 = first_argmax([jnp.where(i1 == r, -jnp.inf, es[r]) for r in range(4)])
    a = jnp.minimum(i1, i2)
    b = jnp.maximum(i1, i2)
    cls = g * 6 + jnp.where(a == 0, b - 1, jnp.where(a == 1, b + 1, 5))

    onehot = lax.broadcasted_iota(jnp.int32, (CLASS_ROWS, tm), 0) == cls
    oh = jnp.where(onehot, 1.0, 0.0)
    before = jnp.dot(oh.astype(BF16), u_ref[...], preferred_element_type=F32) + carry_ref[...]
    rank = jnp.sum(jnp.where(onehot, before, 0.0), axis=0, keepdims=True)
    carry_ref[...] += jnp.sum(oh, axis=1, keepdims=True)
    cls_ref[0] = cls
    rank_ref[0] = rank.astype(jnp.int32)
    cnt_ref[...] = jnp.broadcast_to(carry_ref[...], cnt_ref.shape).astype(jnp.int32)


def _mix_cross_router(x2, ya, yb, cu, cv, sgu_g, w_s, b_full, ncout_g, wmix_bf,
                      nc_g, wq_bf, kmem, vmem, wo_bf, nf_g, wrt_t, brt_col, seq):
    t = x2.shape[0]
    tm = ROW_TILE
    nt = t // tm
    per_b = seq // tm
    u = (lax.broadcasted_iota(jnp.int32, (tm, tm), 0) < lax.broadcasted_iota(jnp.int32, (tm, tm), 1)).astype(BF16)
    row = pl.BlockSpec((tm, D_MODEL), lambda i: (i, 0))
    roww = lambda w: pl.BlockSpec((tm, w), lambda i: (i, 0))
    full = lambda a: pl.BlockSpec(a.shape, lambda i: (0,) * a.ndim)
    mspec = pl.BlockSpec((1, MEM_LEN, CROSS_WIDTH), lambda i: (i // per_b, 0, 0))
    tokspec = pl.BlockSpec((1, 1, tm), lambda i: (i, 0, 0))
    return pl.pallas_call(
        functools.partial(_cross_kernel, tm=tm),
        grid=(nt,),
        in_specs=[row, roww(512), roww(256), roww(256), roww(256),
                  full(sgu_g), full(w_s), full(b_full), full(ncout_g), full(wmix_bf),
                  full(nc_g), full(wq_bf), mspec, mspec, full(wo_bf), full(nf_g), full(wrt_t),
                  full(brt_col), full(u)],
        out_specs=[row, tokspec, tokspec, pl.BlockSpec((CLASS_ROWS, 128), lambda i: (0, 0))],
        out_shape=[jax.ShapeDtypeStruct(x2.shape, F32), jax.ShapeDtypeStruct((nt, 1, tm), jnp.int32),
                   jax.ShapeDtypeStruct((nt, 1, tm), jnp.int32),
                   jax.ShapeDtypeStruct((CLASS_ROWS, 128), jnp.int32)],
        scratch_shapes=[pltpu.VMEM((CLASS_ROWS, 1), F32)],
        compiler_params=pltpu.CompilerParams(dimension_semantics=("arbitrary",), vmem_limit_bytes=VMEM_LIMIT),
        name="mix_cross_router",
    )(x2, ya, yb, cu, cv, sgu_g, w_s, b_full, ncout_g, wmix_bf,
      nc_g, wq_bf, kmem, vmem, wo_bf, nf_g, wrt_t, brt_col, u)


def _moe_kernel(e1_ref, e2_ref, nv_ref, src_ref,
                x_hbm, nf_ref, wrt_ref, brt_ref, wg1_ref, wu1_ref, wd1_ref, wg2_ref, wu2_ref, wd2_ref,
                nfin_ref, out_hbm, xbuf, obuf, wgc, wuc, wdc, gsem, ssem, *, tile, n_tiles, final):
    k = pl.program_id(0)
    slot = k % 2
    prev = jnp.maximum(k - 1, 0)

    def gather_row(kk, sl, r):
        tok = src_ref[kk * tile + r]
        return pltpu.make_async_copy(x_hbm.at[pl.ds(tok, 1), :], xbuf.at[sl, pl.ds(r, 1), :], gsem.at[sl])

    def scatter_row(kk, sl, r):
        tok = src_ref[kk * tile + r]
        return pltpu.make_async_copy(obuf.at[sl, pl.ds(r, 1), :], out_hbm.at[pl.ds(tok, 1), :], ssem.at[sl])

    def start_gather(kk, sl):
        def body8(g, c):
            for r in range(8):
                gather_row(kk, sl, g * 8 + r).start()
            return c
        lax.fori_loop(0, tile // 8, body8, 0)

    def wait_gather(sl):
        pltpu.make_async_copy(x_hbm.at[pl.ds(0, tile), :], xbuf.at[sl], gsem.at[sl]).wait()

    def wait_scatter(n, sl):
        n8 = pl.multiple_of((n // 8) * 8, 8)

        @pl.when(n8 > 0)
        def _():
            pltpu.make_async_copy(obuf.at[sl, pl.ds(0, n8), :], out_hbm.at[pl.ds(0, n8), :], ssem.at[sl]).wait()

        def body(r, c):
            pltpu.make_async_copy(obuf.at[sl, pl.ds(0, 1), :], out_hbm.at[pl.ds(0, 1), :], ssem.at[sl]).wait()
            return c
        lax.fori_loop(n8, n, body, 0)

    @pl.when(jnp.logical_and(k == 0, nv_ref[0] > 0))
    def _():
        start_gather(0, 0)

    @pl.when(k >= 2)
    def _():
        wait_scatter(nv_ref[jnp.maximum(k - 2, 0)], slot)

    nxt = jnp.minimum(k + 1, n_tiles - 1)

    @pl.when(jnp.logical_and(k + 1 < n_tiles, nv_ref[nxt] > 0))
    def _():
        start_gather(k + 1, 1 - slot)

    @pl.when(nv_ref[k] > 0)
    def _():
        @pl.when(jnp.logical_or(k == 0, e1_ref[k] != e1_ref[prev]))
        def _():
            wgc[0] = wg1_ref[0, 0].astype(BF16)
            wuc[0] = wu1_ref[0, 0].astype(BF16)
            wdc[0] = wd1_ref[0, 0].astype(BF16)

        @pl.when(jnp.logical_or(k == 0, e2_ref[k] != e2_ref[prev]))
        def _():
            wgc[1] = wg2_ref[0, 0].astype(BF16)
            wuc[1] = wu2_ref[0, 0].astype(BF16)
            wdc[1] = wd2_ref[0, 0].astype(BF16)

        wait_gather(slot)
        x = xbuf[slot]
        hf = _rms(x, nf_ref[...])
        h = hf.astype(BF16)
        w_hi, w_lo = _split_bf16(wrt_ref[...])
        h_lo = (hf - h.astype(F32)).astype(BF16)
        lg = (jnp.dot(h, w_hi, preferred_element_type=F32) + jnp.dot(h, w_lo, preferred_element_type=F32)
              + jnp.dot(h_lo, w_hi, preferred_element_type=F32)) + brt_ref[...]
        lane = lax.broadcasted_iota(jnp.int32, lg.shape, 1)
        e1 = e1_ref[k]
        e2 = e2_ref[k]
        glm = jnp.where(lane < N_EXPERT_GROUPS, lg, -jnp.inf)
        eg = jnp.exp(glm - jnp.max(glm, axis=-1, keepdims=True))
        pick = lambda a, idx: jnp.sum(jnp.where(lane == idx, a, 0.0), axis=-1, keepdims=True)
        p_group = pick(eg, e1 // EXPERTS_PER_GROUP) / jnp.sum(eg, axis=-1, keepdims=True)
        la = pick(lg, N_EXPERT_GROUPS + e1)
        lb = pick(lg, N_EXPERT_GROUPS + e2)
        mx = jnp.maximum(la, lb)
        ea = jnp.exp(la - mx)
        eb = jnp.exp(lb - mx)
        wa = ea / (ea + eb) * p_group
        wb = eb / (ea + eb) * p_group

        def expert(s):
            gt = jnp.dot(h, wgc[s], preferred_element_type=F32)
            up = jnp.dot(h, wuc[s], preferred_element_type=F32)
            act = (jax.nn.silu(gt) * up).astype(BF16)
            return jnp.dot(act, wdc[s], preferred_element_type=F32)

        y = x + (wa * expert(0) + wb * expert(1))
        if final:
            y = _rms(y, nfin_ref[...])
        obuf[slot] = y

        n_rows = nv_ref[k]

        def body8(g, c):
            for r in range(8):
                scatter_row(k, slot, g * 8 + r).start()
            return c
        lax.fori_loop(0, n_rows // 8, body8, 0)

        def body(r, c):
            scatter_row(k, slot, r).start()
            return c
        lax.fori_loop((n_rows // 8) * 8, n_rows, body, 0)

    @pl.when(k == n_tiles - 1)
    def _():
        wait_scatter(nv_ref[k], slot)
        wait_scatter(nv_ref[prev], 1 - slot)


def _moe(x2, e1, e2, nv, src, nf_g, wrt, brt_row, w_gate, w_up, w_down, layer, nfin_g, final):
    t = x2.shape[0]
    tile = MOE_TILE
    n_tiles = e1.shape[0]
    full = lambda a: pl.BlockSpec(a.shape, lambda k, *_: (0,) * a.ndim)
    wspec1 = lambda a: pl.BlockSpec((1, 1) + a.shape[2:], lambda k, e1r, e2r, nvr, srcr: (layer, e1r[k], 0, 0))
    wspec2 = lambda a: pl.BlockSpec((1, 1) + a.shape[2:], lambda k, e1r, e2r, nvr, srcr: (layer, e2r[k], 0, 0))
    anyspec = pl.BlockSpec(memory_space=pl.ANY)
    grid_spec = pltpu.PrefetchScalarGridSpec(
        num_scalar_prefetch=4,
        grid=(n_tiles,),
        in_specs=[anyspec, full(nf_g), full(wrt), full(brt_row),
                  wspec1(w_gate), wspec1(w_up), wspec1(w_down),
                  wspec2(w_gate), wspec2(w_up), wspec2(w_down), full(nfin_g)],
        out_specs=anyspec,
        scratch_shapes=[pltpu.VMEM((2, tile, D_MODEL), F32), pltpu.VMEM((2, tile, D_MODEL), F32),
                        pltpu.VMEM((2, D_MODEL, D_EXPERT), BF16), pltpu.VMEM((2, D_MODEL, D_EXPERT), BF16),
                        pltpu.VMEM((2, D_EXPERT, D_MODEL), BF16),
                        pltpu.SemaphoreType.DMA((2,)), pltpu.SemaphoreType.DMA((2,))],
    )
    return pl.pallas_call(
        functools.partial(_moe_kernel, tile=tile, n_tiles=n_tiles, final=final),
        grid_spec=grid_spec,
        out_shape=jax.ShapeDtypeStruct((t, D_MODEL), F32),
        compiler_params=pltpu.CompilerParams(dimension_semantics=("arbitrary",), vmem_limit_bytes=VMEM_LIMIT),
        name="moe_experts",
    )(e1, e2, nv, src, x2, nf_g, wrt, brt_row, w_gate, w_up, w_down, w_gate, w_up, w_down, nfin_g)


def _route_meta(cls, rank, counts, t, tile, n_tiles):
    tiles_per = (counts + tile - 1) // tile
    tile_end = jnp.cumsum(tiles_per)
    tile_start = tile_end - tiles_per
    pos = tile_start[cls] * tile + rank
    src = jnp.zeros((n_tiles * tile,), jnp.int32).at[pos].set(jnp.arange(t, dtype=jnp.int32), unique_indices=True)
    kk = jnp.arange(n_tiles, dtype=jnp.int32)
    tcls = jnp.sum((kk[:, None] >= tile_end[None, :]).astype(jnp.int32), axis=1)
    valid = tcls < N_CLASSES
    last_cls = tcls[jnp.maximum(tile_end[-1] - 1, 0)]
    tc = jnp.minimum(jnp.where(valid, tcls, last_cls), N_CLASSES - 1)
    nv = jnp.where(valid, jnp.clip(counts[tc] - (kk - tile_start[tc]) * tile, 0, tile), 0)
    grp = tc // 6
    e1 = grp * EXPERTS_PER_GROUP + jnp.asarray(PAIR_LO, jnp.int32)[tc % 6]
    e2 = grp * EXPERTS_PER_GROUP + jnp.asarray(PAIR_HI, jnp.int32)[tc % 6]
    return e1.astype(jnp.int32), e2.astype(jnp.int32), nv.astype(jnp.int32), src


def _rotary_lane_tables(positions):
    lane = jnp.arange(128, dtype=jnp.int32) % DIFF_QK_DIM
    half = ROPE_DIM // 2
    inv_freq = ROPE_THETA ** (-(2 * (lane % half)).astype(F32) / ROPE_DIM)
    inv_freq = jnp.where(lane < ROPE_DIM, inv_freq, 0.0)
    sign = jnp.where(lane < half, -1.0, 1.0).astype(F32)
    ang = positions.astype(F32).reshape(-1, 1) * inv_freq[None, :]
    return jnp.cos(ang), jnp.sin(ang) * sign[None, :]


def kernel(x, mem, positions, norm_mix, w_in, lam_q1, lam_k1, lam_q2, lam_k2, subln_a, norm_b_out, sgu_norm, w_spatial, b_spatial, norm_c_out, w_out, norm_cross, norm_mem, w_cq, w_ck, w_cv, w_co, norm_ffn, w_group, b_group, w_router, b_router, w_gate, w_up, w_down, norm_final):
    batch, seq, d = x.shape
    depth = w_in.shape[0]
    t = batch * seq
    n_tiles = t // MOE_TILE + N_CLASSES
    x2 = x.reshape(t, d)
    rc, rs = _rotary_lane_tables(positions)
    kmem, vmem = _memkv(mem, norm_mem, w_ck, w_cv)
    row = lambda a: a.reshape(1, -1)
    for l in range(depth):
        lam_init = 0.8 - 0.6 * math.exp(-0.3 * l)
        w_vt = jnp.concatenate([w_in[l][:, 1024:1536], w_in[l][:, 2048:2304]], axis=1).T.astype(BF16)
        aq, ak, avt, bq, bk, bvt, cu, cv = _inproj(x2, row(norm_mix[l]), w_in[l].astype(BF16), w_vt, rc, rs)
        lam_p = jnp.stack([lam_q1[l], lam_k1[l], lam_q2[l], lam_k2[l]])
        ya = _diff_attention(aq, ak, avt, lam_p, row(subln_a[l]), lam_init, batch, seq)
        yb = _stick_breaking(bq, bk, bvt, row(norm_b_out[l]), batch, seq)
        b_full = jnp.repeat(b_spatial[l].T, 64, axis=1)
        wrt = jnp.concatenate([w_group[l], w_router[l],
                               jnp.zeros((d, ROUTER_LANES - N_EXPERT_GROUPS - N_EXPERTS), F32)], axis=1)
        brt = jnp.concatenate([b_group[l], b_router[l],
                               jnp.zeros((ROUTER_LANES - N_EXPERT_GROUPS - N_EXPERTS,), F32)])
        x2, cls, rank, cnt = _mix_cross_router(
            x2, ya, yb, cu, cv, row(sgu_norm[l]), w_spatial[l], b_full, row(norm_c_out[l]), w_out[l].astype(BF16),
            row(norm_cross[l]), w_cq[l].astype(BF16), kmem[l], vmem[l], w_co[l].astype(BF16), row(norm_ffn[l]),
            wrt.T, brt.reshape(-1, 1), seq)
        e1, e2, nv, src = _route_meta(cls.reshape(t), rank.reshape(t), cnt[:N_CLASSES, 0], t, MOE_TILE, n_tiles)
        x2 = _moe(x2, e1, e2, nv, src, row(norm_ffn[l]), wrt, brt.reshape(1, -1), w_gate, w_up, w_down, l,
                  row(norm_final), final=(l == depth - 1))
    return x2.reshape(batch, seq, d)
```
